```python
import jax, jax.numpy as jnp
from jax import lax
import numpy as np

D_MODEL = 2048
BATCH = 2
SEQ = 4096
DEPTH = 4
DEC_BATCH = 8
DEC_SEQ = 1
PAST_LEN = 16384
PAGE_SIZE = 128

N_META = 16
N_A_LAYERS = DEPTH // 2
N_B_LAYERS = DEPTH - N_A_LAYERS
CONV_A_WIDTH = 31
FFN_CONV_WIDTH = 3
D_FF = ((8 * D_MODEL // 3 + 255) // 256) * 256
HEAD_DIM = 128
N_HEADS = D_MODEL // HEAD_DIM
BLOCK_Q = 128
EPS = 1e-6
SB_BIAS_INIT = -6.0

kernel_name = "yoco_conformer_stickbreaking_decoder_step"


def _rmsnorm(x, g):
    xf = x.astype(jnp.float32)
    y = xf * lax.rsqrt(jnp.mean(xf * xf, axis=-1, keepdims=True) + EPS)
    return (y * g.astype(jnp.float32)).astype(x.dtype)


def _causal_dwconv(u, past, w, b):
    full = jnp.concatenate([past.astype(u.dtype), u], axis=1)
    out = lax.conv_general_dilated(
        full, w.astype(u.dtype)[:, None, :], window_strides=(1,), padding="VALID",
        dimension_numbers=("NWC", "WIO", "NWC"), feature_group_count=u.shape[-1])
    return out + b.astype(u.dtype), full[:, -(w.shape[0] - 1):]


def _conformer_conv(x, past, g_norm, w_pw1, w_dw, b_dw, g_cn, w_pw2):
    h = _rmsnorm(x, g_norm)
    a, gt = jnp.split(h @ w_pw1, 2, axis=-1)
    u = a * jax.nn.sigmoid(gt)
    c, new_past = _causal_dwconv(u, past, w_dw, b_dw)
    c = jax.nn.silu(_rmsnorm(c, g_cn))
    return x + c @ w_pw2, new_past


def _conv_ffn(x, past, g_norm, w_gate, w_up, w_down, w_conv, b_conv):
    h = _rmsnorm(x, g_norm)
    gate_c, new_past = _causal_dwconv(h @ w_gate, past, w_conv, b_conv)
    return x + (jax.nn.silu(gate_c) * (h @ w_up)) @ w_down, new_past


def _shared_kv(h, g_norm, w_kv, g_k):
    b, t, _ = h.shape
    k, v = jnp.split(_rmsnorm(h, g_norm) @ w_kv, 2, axis=-1)
    k = _rmsnorm(k.reshape(b, t, N_HEADS, HEAD_DIM), g_k)
    return k, v.reshape(b, t, N_HEADS, HEAD_DIM)


def _sb_block(q, q_pos, k, v, k_pos, sb_bias):
    z = jnp.einsum("bqhd,bkhd->bhqk", q.astype(jnp.float32), k.astype(jnp.float32)) * (HEAD_DIM ** -0.5)
    z = z + sb_bias.astype(jnp.float32)[None, :, None, None]
    mask = k_pos[None, :] < q_pos[:, None]
    log_1mb = jnp.where(mask, jax.nn.log_sigmoid(-z), 0.0)
    suffix = lax.cumsum(log_1mb, axis=3, reverse=True) - log_1mb
    att = jnp.where(mask, jnp.exp(jax.nn.log_sigmoid(z) + suffix), 0.0)
    return jnp.einsum("bhqk,bkhd->bqhd", att, v.astype(jnp.float32)).astype(q.dtype)


def _sb_attend(q, q_pos, k, v, k_pos, sb_bias, lead):
    parts = []
    if lead > 0:
        parts.append(_sb_block(q[:, :lead], q_pos[:lead], k, v, k_pos, sb_bias))
    qr, pr = q[:, lead:], q_pos[lead:]
    b, tq, h, d = qr.shape
    if tq % BLOCK_Q != 0:
        parts.append(_sb_block(qr, pr, k, v, k_pos, sb_bias))
    else:
        nb = tq // BLOCK_Q
        qb = qr.reshape(b, nb, BLOCK_Q, h, d).transpose(1, 0, 2, 3, 4)
        pb = pr.reshape(nb, BLOCK_Q)
        ob = lax.map(lambda a: _sb_block(a[0], a[1], k, v, k_pos, sb_bias), (qb, pb))
        parts.append(ob.transpose(1, 0, 2, 3, 4).reshape(b, tq, h, d))
    return jnp.concatenate(parts, axis=1) if len(parts) > 1 else parts[0]


def _sb_layer(x, q_pos, k, v, k_pos, lead, g_norm, w_q, g_q, sb_bias, w_o):
    b, t, _ = x.shape
    q = _rmsnorm((_rmsnorm(x, g_norm) @ w_q).reshape(b, t, N_HEADS, HEAD_DIM), g_q)
    o = _sb_attend(q, q_pos, k, v, k_pos, sb_bias, lead)
    return x + o.reshape(b, t, D_MODEL) @ w_o


def _trunk(x, past_conv_a, past_ffn, past_k, past_v, lead,
           a_g_norm, a_w_pw1, a_w_dw, a_b_dw, a_g_cn, a_w_pw2,
           f_g_norm, f_w_gate, f_w_up, f_w_down, f_w_conv, f_b_conv,
           kv_g_norm, w_kv, g_k, b_g_norm, b_w_q, b_g_q, b_sb, b_w_o):
    n_past = past_k.shape[1]
    t_new = x.shape[1]
    q_pos = n_past + jnp.arange(t_new, dtype=jnp.int32)
    k_pos = jnp.arange(n_past + t_new, dtype=jnp.int32)
    new_conv_a, new_ffn = [], []
    k_new = v_new = k_all = v_all = None
    for l in range(DEPTH):
        if l < N_A_LAYERS:
            x, s = _conformer_conv(x, past_conv_a[l], a_g_norm[l], a_w_pw1[l], a_w_dw[l],
                                   a_b_dw[l], a_g_cn[l], a_w_pw2[l])
            new_conv_a.append(s)
        else:
            j = l - N_A_LAYERS
            x = _sb_layer(x, q_pos, k_all, v_all, k_pos, lead,
                          b_g_norm[j], b_w_q[j], b_g_q[j], b_sb[j], b_w_o[j])
        x, s = _conv_ffn(x, past_ffn[l], f_g_norm[l], f_w_gate[l], f_w_up[l], f_w_down[l],
                         f_w_conv[l], f_b_conv[l])
        new_ffn.append(s)
        if l == N_A_LAYERS - 1:
            k_new, v_new = _shared_kv(x, kv_g_norm, w_kv, g_k)
            k_all = jnp.concatenate([past_k.astype(k_new.dtype), k_new], axis=1)
            v_all = jnp.concatenate([past_v.astype(v_new.dtype), v_new], axis=1)
    return x, jnp.stack(new_conv_a), jnp.stack(new_ffn), k_new, v_new


def setup_inputs(seed: int = 0) -> dict:
    key = jax.random.key(seed)
    ks = iter(jax.random.split(key, 32))

    def nrm(shape, scale):
        return jax.random.normal(next(ks), shape, jnp.float32) * scale

    def gain(shape):
        return 1.0 + nrm(shape, 0.02)

    n_pages = PAST_LEN // PAGE_SIZE
    n_used = DEC_BATCH * n_pages
    n_pool = n_used + max(1, n_used // 4)
    x_prompt = nrm((BATCH, SEQ, D_MODEL), 1.0)
    x_sample = nrm((DEC_BATCH, DEC_SEQ, D_MODEL), 1.0)
    state_conv_a = nrm((N_A_LAYERS, DEC_BATCH, CONV_A_WIDTH - 1, D_MODEL), 1.0)
    state_ffn_conv = nrm((DEPTH, DEC_BATCH, FFN_CONV_WIDTH - 1, D_FF), 1.0)
    cache_k = nrm((n_pool, PAGE_SIZE, N_HEADS, HEAD_DIM), 1.0)
    cache_v = nrm((n_pool, PAGE_SIZE, N_HEADS, HEAD_DIM), 1.0)
    perm = jax.random.permutation(next(ks), n_pool)
    page_table = perm[:n_used].reshape(DEC_BATCH, n_pages).astype(jnp.int32)
    return {
        "x_prompt": x_prompt,
        "x_sample": x_sample,
        "state_conv_a": state_conv_a,
        "state_ffn_conv": state_ffn_conv,
        "cache_k": cache_k,
        "cache_v": cache_v,
        "page_table": page_table,
        "meta_tokens": nrm((N_META, D_MODEL), 1.0),
        "a_g_norm": gain((N_A_LAYERS, D_MODEL)),
        "a_w_pw1": nrm((N_A_LAYERS, D_MODEL, 2 * D_MODEL), D_MODEL ** -0.5),
        "a_w_dw": nrm((N_A_LAYERS, CONV_A_WIDTH, D_MODEL), CONV_A_WIDTH ** -0.5),
        "a_b_dw": nrm((N_A_LAYERS, D_MODEL), 0.01),
        "a_g_cn": gain((N_A_LAYERS, D_MODEL)),
        "a_w_pw2": nrm((N_A_LAYERS, D_MODEL, D_MODEL), D_MODEL ** -0.5),
        "f_g_norm": gain((DEPTH, D_MODEL)),
        "f_w_gate": nrm((DEPTH, D_MODEL, D_FF), D_MODEL ** -0.5),
        "f_w_up": nrm((DEPTH, D_MODEL, D_FF), D_MODEL ** -0.5),
        "f_w_down": nrm((DEPTH, D_FF, D_MODEL), D_FF ** -0.5),
        "f_w_conv": nrm((DEPTH, FFN_CONV_WIDTH, D_FF), FFN_CONV_WIDTH ** -0.5),
        "f_b_conv": nrm((DEPTH, D_FF), 0.01),
        "kv_g_norm": gain((D_MODEL,)),
        "w_kv": nrm((D_MODEL, 2 * D_MODEL), D_MODEL ** -0.5),
        "g_k": gain((HEAD_DIM,)),
        "b_g_norm": gain((N_B_LAYERS, D_MODEL)),
        "b_w_q": nrm((N_B_LAYERS, D_MODEL, D_MODEL), D_MODEL ** -0.5),
        "b_g_q": gain((N_B_LAYERS, HEAD_DIM)),
        "b_sb": SB_BIAS_INIT + nrm((N_B_LAYERS, N_HEADS), 0.1),
        "b_w_o": nrm((N_B_LAYERS, D_MODEL, D_MODEL), D_MODEL ** -0.5),
    }


def reference(x_prompt, x_sample, state_conv_a, state_ffn_conv, cache_k, cache_v, page_table,
              meta_tokens, a_g_norm, a_w_pw1, a_w_dw, a_b_dw, a_g_cn, a_w_pw2,
              f_g_norm, f_w_gate, f_w_up, f_w_down, f_w_conv, f_b_conv,
              kv_g_norm, w_kv, g_k, b_g_norm, b_w_q, b_g_q, b_sb, b_w_o):
    weights = (a_g_norm, a_w_pw1, a_w_dw, a_b_dw, a_g_cn, a_w_pw2,
               f_g_norm, f_w_gate, f_w_up, f_w_down, f_w_conv, f_b_conv,
               kv_g_norm, w_kv, g_k, b_g_norm, b_w_q, b_g_q, b_sb, b_w_o)
    dt = x_prompt.dtype
    bp = x_prompt.shape[0]
    meta = jnp.broadcast_to(meta_tokens.astype(dt)[None], (bp, N_META, D_MODEL))
    xp = jnp.concatenate([meta, x_prompt], axis=1)
    zero_conv = jnp.zeros((N_A_LAYERS, bp, CONV_A_WIDTH - 1, D_MODEL), dt)
    zero_ffn = jnp.zeros((DEPTH, bp, FFN_CONV_WIDTH - 1, D_FF), dt)
    zero_kv = jnp.zeros((bp, 0, N_HEADS, HEAD_DIM), dt)
    hp, conv_p, ffn_p, k_p, v_p = _trunk(xp, zero_conv, zero_ffn, zero_kv, zero_kv, N_META, *weights)
    y_prompt = hp[:, N_META:]
    bs, n_pages = page_table.shape
    past_k = cache_k[page_table].reshape(bs, n_pages * PAGE_SIZE, N_HEADS, HEAD_DIM)
    past_v = cache_v[page_table].reshape(bs, n_pages * PAGE_SIZE, N_HEADS, HEAD_DIM)
    y_sample, conv_s, ffn_s, k_s, v_s = _trunk(x_sample, state_conv_a, state_ffn_conv,
                                               past_k, past_v, 0, *weights)
    return (y_prompt, y_sample, conv_p, ffn_p, k_p, v_p, conv_s, ffn_s, k_s, v_s)
```

```python
import functools

import jax
import jax.numpy as jnp
from jax import lax
from jax.experimental import pallas as pl
from jax.experimental.pallas import tpu as pltpu

D_MODEL = 2048
D_FF = 5632
N_HEADS = 16
HEAD_DIM = 128
N_META = 16
N_A_LAYERS = 2
DEPTH = 4
CONV_A_WIDTH = 31
FFN_CONV_WIDTH = 3
PAGE_SIZE = 128
EPS = 1e-6

LANES = 128
SUBLANES = 8
VMEM_LIMIT = 56 * 1024 * 1024

F32 = jnp.float32
BF16 = jnp.bfloat16


def _cparams(sem):
    return pltpu.CompilerParams(dimension_semantics=sem, vmem_limit_bytes=VMEM_LIMIT)


def _rms_rows(x, g):
    ms = jnp.mean(x * x, axis=-1, keepdims=True)
    return x * lax.rsqrt(ms + EPS) * g


def _head_rms(t, gh):
    parts = []
    for s in range(t.shape[1] // HEAD_DIM):
        parts.append(_rms_rows(t[:, s * HEAD_DIM:(s + 1) * HEAD_DIM], gh))
    return parts


def _neg_softplus(z):
    return -(jnp.maximum(z, 0.0) + jnp.log1p(jnp.exp(-jnp.abs(z))))


def _split_bf16(x):
    hi = x.astype(BF16)
    lo = (x - hi.astype(F32)).astype(BF16)
    return hi, lo


def _glu_kernel(x_ref, g_ref, wa_ref, wg_ref, u_ref, h_scr):
    @pl.when(pl.program_id(1) == 0)
    def _():
        h_scr[...] = _rms_rows(x_ref[...], g_ref[...]).astype(BF16)

    h = h_scr[...]
    a = jnp.dot(h, wa_ref[...], preferred_element_type=F32)
    gt = jnp.dot(h, wg_ref[...], preferred_element_type=F32)
    u_ref[...] = a * jax.nn.sigmoid(gt)


def _glu(x, g, w_bf, tm, tn):
    m = x.shape[0]
    nj = D_MODEL // tn
    return pl.pallas_call(
        _glu_kernel,
        grid=(m // tm, nj),
        in_specs=[
            pl.BlockSpec((tm, D_MODEL), lambda i, j: (i, 0)),
            pl.BlockSpec((1, D_MODEL), lambda i, j: (0, 0)),
            pl.BlockSpec((D_MODEL, tn), lambda i, j: (0, j)),
            pl.BlockSpec((D_MODEL, tn), lambda i, j: (0, j + nj)),
        ],
        out_specs=pl.BlockSpec((tm, tn), lambda i, j: (i, j)),
        out_shape=jax.ShapeDtypeStruct((m, D_MODEL), F32),
        scratch_shapes=[pltpu.VMEM((tm, D_MODEL), BF16)],
        compiler_params=_cparams(("parallel", "arbitrary")),
        name="glu",
    )(x, g.reshape(1, D_MODEL), w_bf, w_bf)


DW_HALO = 32


def _dwconv_seq_kernel(u_ref, halo_ref, w_ref, b_ref, o_ref, full_scr, *, tt, cl):
    first = pl.program_id(1) == 0
    full_scr[0:DW_HALO, :] = jnp.where(first, 0.0, halo_ref[0])
    full_scr[DW_HALO:DW_HALO + tt, :] = u_ref[0]
    lead = DW_HALO - (CONV_A_WIDTH - 1)
    for c in range(cl // LANES):
        ls = slice(c * LANES, (c + 1) * LANES)
        acc = jnp.broadcast_to(b_ref[:, ls], (tt, LANES))
        for k in range(CONV_A_WIDTH):
            acc = acc + w_ref[k:k + 1, ls] * full_scr[lead + k:lead + k + tt, ls]
        o_ref[0, :, ls] = acc


def _dwconv_seq(u3, w, b, tt=128, cl=512):
    bsz, tp, _ = u3.shape
    w_pad = jnp.zeros((DW_HALO, D_MODEL), F32).at[:CONV_A_WIDTH].set(w)
    hb = tt // DW_HALO
    return pl.pallas_call(
        functools.partial(_dwconv_seq_kernel, tt=tt, cl=cl),
        grid=(bsz, tp // tt, D_MODEL // cl),
        in_specs=[
            pl.BlockSpec((1, tt, cl), lambda bb, i, c: (bb, i, c)),
            pl.BlockSpec((1, DW_HALO, cl), lambda bb, i, c: (bb, jnp.maximum(i * hb - 1, 0), c)),
            pl.BlockSpec((DW_HALO, cl), lambda bb, i, c: (0, c)),
            pl.BlockSpec((1, cl), lambda bb, i, c: (0, c)),
        ],
        out_specs=pl.BlockSpec((1, tt, cl), lambda bb, i, c: (bb, i, c)),
        out_shape=jax.ShapeDtypeStruct(u3.shape, F32),
        scratch_shapes=[pltpu.VMEM((DW_HALO + tt, cl), F32)],
        compiler_params=_cparams(("parallel", "parallel", "parallel")),
        name="dwconv_seq",
    )(u3, u3, w_pad, b.reshape(1, D_MODEL))


def _dwconv_step_kernel(s_ref, u_ref, w_ref, b_ref, o_ref):
    w = w_ref[...]
    past = jnp.sum(s_ref[...] * w[None, :CONV_A_WIDTH - 1, :], axis=1)
    o_ref[...] = past + u_ref[...] * w[CONV_A_WIDTH - 1:CONV_A_WIDTH, :] + b_ref[...]


def _dwconv_step(state, u, w, b):
    nb = state.shape[0]
    return pl.pallas_call(
        _dwconv_step_kernel,
        out_shape=jax.ShapeDtypeStruct((nb, D_MODEL), F32),
        compiler_params=pltpu.CompilerParams(vmem_limit_bytes=VMEM_LIMIT),
        name="dwconv_step",
    )(state, u, w, b.reshape(1, D_MODEL))


def _norm_silu_mm_res_kernel(c_ref, g_ref, w_ref, x_ref, o_ref, h_scr):
    @pl.when(pl.program_id(1) == 0)
    def _():
        y = _rms_rows(c_ref[...], g_ref[...])
        h_scr[...] = (y * jax.nn.sigmoid(y)).astype(BF16)

    o_ref[...] = x_ref[...] + jnp.dot(h_scr[...], w_ref[...], preferred_element_type=F32)


def _norm_silu_mm_res(c, g, w_bf, xres, tm, tn):
    m = c.shape[0]
    return pl.pallas_call(
        _norm_silu_mm_res_kernel,
        grid=(m // tm, D_MODEL // tn),
        in_specs=[
            pl.BlockSpec((tm, D_MODEL), lambda i, j: (i, 0)),
            pl.BlockSpec((1, D_MODEL), lambda i, j: (0, 0)),
            pl.BlockSpec((D_MODEL, tn), lambda i, j: (0, j)),
            pl.BlockSpec((tm, tn), lambda i, j: (i, j)),
        ],
        out_specs=pl.BlockSpec((tm, tn), lambda i, j: (i, j)),
        out_shape=jax.ShapeDtypeStruct((m, D_MODEL), F32),
        scratch_shapes=[pltpu.VMEM((tm, D_MODEL), BF16)],
        compiler_params=_cparams(("parallel", "arbitrary")),
        name="pw2",
    )(c, g.reshape(1, D_MODEL), w_bf, xres)


def _ffn_gate_seq_kernel(x_ref, xp_ref, g_ref, wg_ref, wu_ref, wc_ref, bc_ref,
                         act_ref, tail_ref, h_scr, hp_scr, gs_scr, *, tm, tiles_per_seq, tail_row):
    i = pl.program_id(0)

    @pl.when(pl.program_id(1) == 0)
    def _():
        h_scr[...] = _rms_rows(x_ref[...], g_ref[...]).astype(BF16)
        hp = _rms_rows(xp_ref[...], g_ref[...])
        hp = jnp.where(i % tiles_per_seq == 0, 0.0, hp)
        hp_scr[...] = jnp.concatenate([hp, hp], axis=0).astype(BF16)

    h = h_scr[...]
    wg = wg_ref[...]
    gate = jnp.dot(h, wg, preferred_element_type=F32)
    gprev = jnp.dot(hp_scr[...], wg, preferred_element_type=F32)
    up = jnp.dot(h, wu_ref[...], preferred_element_type=F32)
    gs_scr[0:SUBLANES, :] = gprev[0:SUBLANES]
    gs_scr[SUBLANES:SUBLANES + tm, :] = gate
    g1 = gs_scr[SUBLANES - 1:SUBLANES - 1 + tm, :]
    g2 = gs_scr[SUBLANES - 2:SUBLANES - 2 + tm, :]
    gc = wc_ref[0:1, :] * g2 + wc_ref[1:2, :] * g1 + wc_ref[2:3, :] * gate + bc_ref[...]
    act_ref[...] = (gc * jax.nn.sigmoid(gc) * up).astype(BF16)
    tail_ref[0] = gate[tail_row:tail_row + SUBLANES]


def _ffn_gate_seq(x, g, wg_bf, wu_bf, wc, bc, tm, tn, tiles_per_seq, tail_row):
    m = x.shape[0]
    nt = m // tm
    hb = tm // SUBLANES
    wc_pad = jnp.zeros((SUBLANES, D_FF), F32).at[:FFN_CONV_WIDTH].set(wc)
    return pl.pallas_call(
        functools.partial(_ffn_gate_seq_kernel, tm=tm, tiles_per_seq=tiles_per_seq, tail_row=tail_row),
        grid=(nt, D_FF // tn),
        in_specs=[
            pl.BlockSpec((tm, D_MODEL), lambda i, j: (i, 0)),
            pl.BlockSpec((SUBLANES, D_MODEL), lambda i, j: (jnp.maximum(i * hb - 1, 0), 0)),
            pl.BlockSpec((1, D_MODEL), lambda i, j: (0, 0)),
            pl.BlockSpec((D_MODEL, tn), lambda i, j: (0, j)),
            pl.BlockSpec((D_MODEL, tn), lambda i, j: (0, j)),
            pl.BlockSpec((SUBLANES, tn), lambda i, j: (0, j)),
            pl.BlockSpec((1, tn), lambda i, j: (0, j)),
        ],
        out_specs=[
            pl.BlockSpec((tm, tn), lambda i, j: (i, j)),
            pl.BlockSpec((1, SUBLANES, tn), lambda i, j: (i, 0, j)),
        ],
        out_shape=[
            jax.ShapeDtypeStruct((m, D_FF), BF16),
            jax.ShapeDtypeStruct((nt, SUBLANES, D_FF), F32),
        ],
        scratch_shapes=[
            pltpu.VMEM((tm, D_MODEL), BF16),
            pltpu.VMEM((2 * SUBLANES, D_MODEL), BF16),
            pltpu.VMEM((SUBLANES + tm, tn), F32),
        ],
        compiler_params=_cparams(("parallel", "arbitrary")),
        name="ffn_gate_seq",
    )(x, x, g.reshape(1, D_MODEL), wg_bf, wu_bf, wc_pad, bc.reshape(1, D_FF))


def _ffn_gate_step_kernel(x_ref, g_ref, wg_ref, wu_ref, wc_ref, bc_ref, s0_ref, s1_ref,
                          act_ref, gate_ref, h_scr):
    @pl.when(pl.program_id(0) == 0)
    def _():
        h_scr[...] = _rms_rows(x_ref[...], g_ref[...]).astype(BF16)

    h = h_scr[...]
    gate = jnp.dot(h, wg_ref[...], preferred_element_type=F32)
    up = jnp.dot(h, wu_ref[...], preferred_element_type=F32)
    gc = (wc_ref[0:1, :] * s0_ref[...] + wc_ref[1:2, :] * s1_ref[...]
          + wc_ref[2:3, :] * gate + bc_ref[...])
    act_ref[...] = (gc * jax.nn.sigmoid(gc) * up).astype(BF16)
    gate_ref[...] = gate


def _ffn_gate_step(x, g, wg_bf, wu_bf, wc, bc, s0, s1, tn):
    m = x.shape[0]
    wc_pad = jnp.zeros((SUBLANES, D_FF), F32).at[:FFN_CONV_WIDTH].set(wc)
    return pl.pallas_call(
        _ffn_gate_step_kernel,
        grid=(D_FF // tn,),
        in_specs=[
            pl.BlockSpec((m, D_MODEL), lambda j: (0, 0)),
            pl.BlockSpec((1, D_MODEL), lambda j: (0, 0)),
            pl.BlockSpec((D_MODEL, tn), lambda j: (0, j)),
            pl.BlockSpec((D_MODEL, tn), lambda j: (0, j)),
            pl.BlockSpec((SUBLANES, tn), lambda j: (0, j)),
            pl.BlockSpec((1, tn), lambda j: (0, j)),
            pl.BlockSpec((m, tn), lambda j: (0, j)),
            pl.BlockSpec((m, tn), lambda j: (0, j)),
        ],
        out_specs=[
            pl.BlockSpec((m, tn), lambda j: (0, j)),
            pl.BlockSpec((m, tn), lambda j: (0, j)),
        ],
        out_shape=[
            jax.ShapeDtypeStruct((m, D_FF), BF16),
            jax.ShapeDtypeStruct((m, D_FF), F32),
        ],
        scratch_shapes=[pltpu.VMEM((m, D_MODEL), BF16)],
        compiler_params=_cparams(("arbitrary",)),
        name="ffn_gate_step",
    )(x, g.reshape(1, D_MODEL), wg_bf, wu_bf, wc_pad, bc.reshape(1, D_FF), s0, s1)


def _mm_res_kernel(a_ref, w_ref, x_ref, o_ref):
    o_ref[...] = x_ref[...] + jnp.dot(a_ref[...], w_ref[...], preferred_element_type=F32)


def _mm_res(a_bf, w_bf, xres, tm, tn, name):
    m, kdim = a_bf.shape
    n = w_bf.shape[1]
    return pl.pallas_call(
        _mm_res_kernel,
        grid=(m // tm, n // tn),
        in_specs=[
            pl.BlockSpec((tm, kdim), lambda i, j: (i, 0)),
            pl.BlockSpec((kdim, tn), lambda i, j: (0, j)),
            pl.BlockSpec((tm, tn), lambda i, j: (i, j)),
        ],
        out_specs=pl.BlockSpec((tm, tn), lambda i, j: (i, j)),
        out_shape=jax.ShapeDtypeStruct((m, n), F32),
        compiler_params=_cparams(("parallel", "parallel")),
        name=name,
    )(a_bf, w_bf, xres)


def _kv_kernel(x_ref, g_ref, wk_ref, wv_ref, gk_ref, k_ref, v_ref, kb_ref, vb_ref, h_scr):
    @pl.when(pl.program_id(1) == 0)
    def _():
        h_scr[...] = _rms_rows(x_ref[...], g_ref[...]).astype(BF16)

    h = h_scr[...]
    kraw = jnp.dot(h, wk_ref[...], preferred_element_type=F32)
    v = jnp.dot(h, wv_ref[...], preferred_element_type=F32)
    for s, kn in enumerate(_head_rms(kraw, gk_ref[...])):
        k_ref[:, s * HEAD_DIM:(s + 1) * HEAD_DIM] = kn
        kb_ref[:, s * HEAD_DIM:(s + 1) * HEAD_DIM] = kn.astype(BF16)
    v_ref[...] = v
    vb_ref[...] = v.astype(BF16)


def _kv_proj(x, g, w_bf, gk, tm, tn):
    m = x.shape[0]
    nj = D_MODEL // tn
    blk = pl.BlockSpec((tm, tn), lambda i, j: (i, j))
    return pl.pallas_call(
        _kv_kernel,
        grid=(m // tm, nj),
        in_specs=[
            pl.BlockSpec((tm, D_MODEL), lambda i, j: (i, 0)),
            pl.BlockSpec((1, D_MODEL), lambda i, j: (0, 0)),
            pl.BlockSpec((D_MODEL, tn), lambda i, j: (0, j)),
            pl.BlockSpec((D_MODEL, tn), lambda i, j: (0, j + nj)),
            pl.BlockSpec((1, HEAD_DIM), lambda i, j: (0, 0)),
        ],
        out_specs=[blk, blk, blk, blk],
        out_shape=[
            jax.ShapeDtypeStruct((m, D_MODEL), F32),
            jax.ShapeDtypeStruct((m, D_MODEL), F32),
            jax.ShapeDtypeStruct((m, D_MODEL), BF16),
            jax.ShapeDtypeStruct((m, D_MODEL), BF16),
        ],
        scratch_shapes=[pltpu.VMEM((tm, D_MODEL), BF16)],
        compiler_params=_cparams(("parallel", "arbitrary")),
        name="kv_proj",
    )(x, g.reshape(1, D_MODEL), w_bf, w_bf, gk.reshape(1, HEAD_DIM))


def _q_kernel(x_ref, g_ref, w_ref, gq_ref, q_ref, h_scr):
    @pl.when(pl.program_id(1) == 0)
    def _():
        h_scr[...] = _rms_rows(x_ref[...], g_ref[...]).astype(BF16)

    qraw = jnp.dot(h_scr[...], w_ref[...], preferred_element_type=F32)
    for s, qn in enumerate(_head_rms(qraw, gq_ref[...])):
        q_ref[:, s * HEAD_DIM:(s + 1) * HEAD_DIM] = (qn * (HEAD_DIM ** -0.5)).astype(BF16)


def _q_proj(x, g, w_bf, gq, tm, tn):
    m = x.shape[0]
    return pl.pallas_call(
        _q_kernel,
        grid=(m // tm, D_MODEL // tn),
        in_specs=[
            pl.BlockSpec((tm, D_MODEL), lambda i, j: (i, 0)),
            pl.BlockSpec((1, D_MODEL), lambda i, j: (0, 0)),
            pl.BlockSpec((D_MODEL, tn), lambda i, j: (0, j)),
            pl.BlockSpec((1, HEAD_DIM), lambda i, j: (0, 0)),
        ],
        out_specs=pl.BlockSpec((tm, tn), lambda i, j: (i, j)),
        out_shape=jax.ShapeDtypeStruct((m, D_MODEL), BF16),
        scratch_shapes=[pltpu.VMEM((tm, D_MODEL), BF16)],
        compiler_params=_cparams(("parallel", "arbitrary")),
        name="q_proj",
    )(x, g.reshape(1, D_MODEL), w_bf, gq.reshape(1, HEAD_DIM))


def _sb_attn_seq_kernel(sb_ref, q_ref, k_ref, v_ref, to_ref, o_ref, acc_scr, run_scr, *, tq, tk):
    h = pl.program_id(1)
    qi = pl.program_id(2)
    bias = sb_ref[h]
    q = q_ref[0]
    to = to_ref[...]
    nkb = (qi + 1) * (tq // tk)
    diff = (lax.broadcasted_iota(jnp.int32, (tq, tk), 0)
            - lax.broadcasted_iota(jnp.int32, (tq, tk), 1))
    acc_scr[...] = jnp.zeros_like(acc_scr)
    run_scr[...] = jnp.zeros_like(run_scr)

    def body(jj, carry):
        j = nkb - 1 - jj
        off = pl.multiple_of(j * tk, tk)
        ks = k_ref[0, pl.ds(off, tk), :]
        vs = v_ref[0, pl.ds(off, tk), :]
        z = lax.dot_general(q, ks, (((1,), (1,)), ((), ())), preferred_element_type=F32) + bias
        mask = diff > (j * tk - qi * tq)
        l = jnp.where(mask, _neg_softplus(z), 0.0)
        l_hi, l_lo = _split_bf16(l)
        cs = (jnp.dot(l_hi, to, preferred_element_type=F32)
              + jnp.dot(l_lo, to, preferred_element_type=F32))
        run = run_scr[...]
        att = jnp.where(mask, jnp.exp(z + cs[:, :tk] + run), 0.0)
        acc_scr[...] += jnp.dot(att.astype(BF16), vs, preferred_element_type=F32)
        run_scr[...] = run + cs[:, tk:]
        return carry

    lax.fori_loop(0, nkb, body, 0)
    o_ref[0] = acc_scr[...].astype(BF16)


def _suffix_and_ones(tk):
    r = lax.broadcasted_iota(jnp.int32, (tk, tk), 0)
    c = lax.broadcasted_iota(jnp.int32, (tk, tk), 1)
    return jnp.concatenate([(r >= c).astype(BF16), jnp.ones((tk, tk), BF16)], axis=1)


def _sb_attn_seq(q3, kb3, vb3, sb, tq=384, tk=128):
    bsz, tp, _ = q3.shape
    grid_spec = pltpu.PrefetchScalarGridSpec(
        num_scalar_prefetch=1,
        grid=(bsz, N_HEADS, tp // tq),
        in_specs=[
            pl.BlockSpec((1, tq, HEAD_DIM), lambda b, h, i, sbr: (b, i, h)),
            pl.BlockSpec((1, tp, HEAD_DIM), lambda b, h, i, sbr: (b, 0, h)),
            pl.BlockSpec((1, tp, HEAD_DIM), lambda b, h, i, sbr: (b, 0, h)),
            pl.BlockSpec((tk, 2 * tk), lambda b, h, i, sbr: (0, 0)),
        ],
        out_specs=pl.BlockSpec((1, tq, HEAD_DIM), lambda b, h, i, sbr: (b, i, h)),
        scratch_shapes=[pltpu.VMEM((tq, HEAD_DIM), F32), pltpu.VMEM((tq, tk), F32)],
    )
    return pl.pallas_call(
        functools.partial(_sb_attn_seq_kernel, tq=tq, tk=tk),
        grid_spec=grid_spec,
        out_shape=jax.ShapeDtypeStruct(q3.shape, BF16),
        compiler_params=_cparams(("parallel", "parallel", "arbitrary")),
        name="sb_attn_seq",
    )(sb, q3, kb3, vb3, _suffix_and_ones(tk))


def _sb_attn_paged_kernel(pt_ref, qbd_ref, bias_ref, k_ref, v_ref, tri_ref, ex_ref, o_ref,
                          acc_scr, run_scr, *, n_pages):
    p = pl.program_id(1)

    @pl.when(p == 0)
    def _():
        acc_scr[...] = jnp.zeros_like(acc_scr)
        run_scr[...] = jnp.zeros_like(run_scr)

    kb = k_ref[0].astype(BF16)
    z = jnp.dot(kb, qbd_ref[0], preferred_element_type=F32) + bias_ref[...]
    l_hi, l_lo = _split_bf16(_neg_softplus(z))
    tri = tri_ref[...]
    cs = (jnp.dot(tri, l_hi, preferred_element_type=F32)
          + jnp.dot(tri, l_lo, preferred_element_type=F32))
    run = run_scr[...]
    att = jnp.exp(z + cs + run)
    run_scr[...] = run + cs[0:1, :]
    attx = jnp.dot(att.astype(BF16), ex_ref[...], preferred_element_type=F32)
    prod = attx * v_ref[0]
    part = prod[0:SUBLANES]
    for r in range(1, PAGE_SIZE // SUBLANES):
        part = part + prod[r * SUBLANES:(r + 1) * SUBLANES]
    acc_scr[...] += part

    @pl.when(p == n_pages - 1)
    def _():
        o_ref[0] = jnp.sum(acc_scr[...], axis=0, keepdims=True)


def _sb_attn_paged(q, cache_k, cache_v, page_table, sb):
    nb, n_pages = page_table.shape
    n_pool = cache_k.shape[0]
    ck = cache_k.reshape(n_pool, PAGE_SIZE, D_MODEL)
    cv = cache_v.reshape(n_pool, PAGE_SIZE, D_MODEL)
    head_of_col = lax.broadcasted_iota(jnp.int32, (D_MODEL, LANES), 0) // HEAD_DIM
    lane = lax.broadcasted_iota(jnp.int32, (D_MODEL, LANES), 1)
    sel = head_of_col == lane
    qbd = jnp.where(sel[None], q[:, :, None], jnp.zeros((), BF16))
    expand = sel.T.astype(BF16)
    bias_row = jnp.zeros((1, LANES), F32).at[0, :N_HEADS].set(sb)
    r = lax.broadcasted_iota(jnp.int32, (PAGE_SIZE, PAGE_SIZE), 0)
    c = lax.broadcasted_iota(jnp.int32, (PAGE_SIZE, PAGE_SIZE), 1)
    tri = (c >= r).astype(BF16)
    grid_spec = pltpu.PrefetchScalarGridSpec(
        num_scalar_prefetch=1,
        grid=(nb, n_pages),
        in_specs=[
            pl.BlockSpec((1, D_MODEL, LANES), lambda b, p, pt: (b, 0, 0)),
            pl.BlockSpec((1, LANES), lambda b, p, pt: (0, 0)),
            pl.BlockSpec((1, PAGE_SIZE, D_MODEL), lambda b, p, pt: (pt[b, n_pages - 1 - p], 0, 0)),
            pl.BlockSpec((1, PAGE_SIZE, D_MODEL), lambda b, p, pt: (pt[b, n_pages - 1 - p], 0, 0)),
            pl.BlockSpec((PAGE_SIZE, PAGE_SIZE), lambda b, p, pt: (0, 0)),
            pl.BlockSpec((LANES, D_MODEL), lambda b, p, pt: (0, 0)),
        ],
        out_specs=pl.BlockSpec((1, 1, D_MODEL), lambda b, p, pt: (b, 0, 0)),
        scratch_shapes=[pltpu.VMEM((SUBLANES, D_MODEL), F32), pltpu.VMEM((1, LANES), F32)],
    )
    o = pl.pallas_call(
        functools.partial(_sb_attn_paged_kernel, n_pages=n_pages),
        grid_spec=grid_spec,
        out_shape=jax.ShapeDtypeStruct((nb, 1, D_MODEL), F32),
        compiler_params=_cparams(("parallel", "arbitrary")),
        name="sb_attn_paged",
    )(page_table, qbd, bias_row, ck, cv, tri, expand)
    return o.reshape(nb, D_MODEL)


T_PAD = 4224
TM_P = 1056
TM_DOWN = 528
TN_NORM = 256
TN_MM = 512
SAMPLE_ROWS = 16


def _prompt_trunk(x, wts):
    bsz, t_real, _ = x.shape
    pad = T_PAD - t_real
    x = jnp.pad(x, ((0, 0), (0, pad), (0, 0))).reshape(bsz * T_PAD, D_MODEL)
    tiles_per_seq = T_PAD // TM_P
    tail_row = (t_real - 1) % TM_P - (SUBLANES - 1)
    assert (t_real - 1) // TM_P == tiles_per_seq - 1 and tail_row % SUBLANES == 0
    conv_states, ffn_states = [], []
    k = v = kb = vb = None
    for l in range(DEPTH):
        if l < N_A_LAYERS:
            u = _glu(x, wts["a_g_norm"][l], wts["a_w_pw1"][l], TM_P, TN_NORM)
            u3 = u.reshape(bsz, T_PAD, D_MODEL)
            conv_states.append(u3[:, t_real - (CONV_A_WIDTH - 1):t_real])
            c = _dwconv_seq(u3, wts["a_w_dw"][l], wts["a_b_dw"][l])
            x = _norm_silu_mm_res(c.reshape(bsz * T_PAD, D_MODEL), wts["a_g_cn"][l],
                                  wts["a_w_pw2"][l], x, TM_P, TN_NORM)
        else:
            j = l - N_A_LAYERS
            q = _q_proj(x, wts["b_g_norm"][j], wts["b_w_q"][j], wts["b_g_q"][j], TM_P, TN_NORM)
            o = _sb_attn_seq(q.reshape(bsz, T_PAD, D_MODEL), kb, vb, wts["b_sb"][j])
            x = _mm_res(o.reshape(bsz * T_PAD, D_MODEL), wts["b_w_o"][j], x, TM_P, TN_MM, "o_proj")
        act, tail = _ffn_gate_seq(x, wts["f_g_norm"][l], wts["f_w_gate"][l], wts["f_w_up"][l],
                                  wts["f_w_conv"][l], wts["f_b_conv"][l], TM_P, TN_NORM,
                                  tiles_per_seq, tail_row)
        tail = tail.reshape(bsz, tiles_per_seq, SUBLANES, D_FF)
        ffn_states.append(tail[:, tiles_per_seq - 1, SUBLANES - (FFN_CONV_WIDTH - 1):])
        x = _mm_res(act, wts["f_w_down"][l], x, TM_DOWN, TN_MM, "ffn_down")
        if l == N_A_LAYERS - 1:
            k, v, kb, vb = _kv_proj(x, wts["kv_g_norm"], wts["w_kv"], wts["g_k"], TM_P, TN_NORM)
            kb = kb.reshape(bsz, T_PAD, D_MODEL)
            vb = vb.reshape(bsz, T_PAD, D_MODEL)
    x = x.reshape(bsz, T_PAD, D_MODEL)[:, N_META:t_real]
    k = k.reshape(bsz, T_PAD, N_HEADS, HEAD_DIM)[:, :t_real]
    v = v.reshape(bsz, T_PAD, N_HEADS, HEAD_DIM)[:, :t_real]
    return x, jnp.stack(conv_states), jnp.stack(ffn_states), k, v


def _sample_trunk(x, state_conv_a, state_ffn, cache_k, cache_v, page_table, wts):
    nb = x.shape[0]
    m = SAMPLE_ROWS

    def pad_rows(a):
        return jnp.pad(a, ((0, m - nb), (0, 0)))

    x = pad_rows(x.reshape(nb, D_MODEL))
    conv_states, ffn_states = [], []
    k = v = None
    for l in range(DEPTH):
        if l < N_A_LAYERS:
            u = _glu(x, wts["a_g_norm"][l], wts["a_w_pw1"][l], m, TN_MM)[:nb]
            past = state_conv_a[l]
            conv_states.append(jnp.concatenate([past[:, 1:], u[:, None]], axis=1))
            c = _dwconv_step(past, u, wts["a_w_dw"][l], wts["a_b_dw"][l])
            x = _norm_silu_mm_res(pad_rows(c), wts["a_g_cn"][l], wts["a_w_pw2"][l], x, m, TN_MM)
        else:
            j = l - N_A_LAYERS
            q = _q_proj(x, wts["b_g_norm"][j], wts["b_w_q"][j], wts["b_g_q"][j], m, TN_MM)
            o = _sb_attn_paged(q[:nb], cache_k, cache_v, page_table, wts["b_sb"][j])
            x = _mm_res(pad_rows(o).astype(BF16), wts["b_w_o"][j], x, m, TN_MM, "o_proj")
        past = state_ffn[l]
        act, gate = _ffn_gate_step(x, wts["f_g_norm"][l], wts["f_w_gate"][l], wts["f_w_up"][l],
                                   wts["f_w_conv"][l], wts["f_b_conv"][l],
                                   pad_rows(past[:, 0]), pad_rows(past[:, 1]), TN_MM)
        ffn_states.append(jnp.stack([past[:, 1], gate[:nb]], axis=1))
        x = _mm_res(act, wts["f_w_down"][l], x, m, TN_MM, "ffn_down")
        if l == N_A_LAYERS - 1:
            k, v, _, _ = _kv_proj(x, wts["kv_g_norm"], wts["w_kv"], wts["g_k"], m, TN_MM)
    y = x[:nb].reshape(nb, 1, D_MODEL)
    k = k[:nb].reshape(nb, 1, N_HEADS, HEAD_DIM)
    v = v[:nb].reshape(nb, 1, N_HEADS, HEAD_DIM)
    return y, jnp.stack(conv_states), jnp.stack(ffn_states), k, v


_MATMUL_WEIGHTS = ("a_w_pw1", "a_w_pw2", "f_w_gate", "f_w_up", "f_w_down", "w_kv", "b_w_q", "b_w_o")


def kernel(x_prompt, x_sample, state_conv_a, state_ffn_conv, cache_k, cache_v, page_table, meta_tokens, a_g_norm, a_w_pw1, a_w_dw, a_b_dw, a_g_cn, a_w_pw2, f_g_norm, f_w_gate, f_w_up, f_w_down, f_w_conv, f_b_conv, kv_g_norm, w_kv, g_k, b_g_norm, b_w_q, b_g_q, b_sb, b_w_o):
    wts = dict(a_g_norm=a_g_norm, a_w_pw1=a_w_pw1, a_w_dw=a_w_dw, a_b_dw=a_b_dw, a_g_cn=a_g_cn,
               a_w_pw2=a_w_pw2, f_g_norm=f_g_norm, f_w_gate=f_w_gate, f_w_up=f_w_up,
               f_w_down=f_w_down, f_w_conv=f_w_conv, f_b_conv=f_b_conv, kv_g_norm=kv_g_norm,
               w_kv=w_kv, g_k=g_k, b_g_norm=b_g_norm, b_w_q=b_w_q, b_g_q=b_g_q, b_sb=b_sb,
               b_w_o=b_w_o)
    for name in _MATMUL_WEIGHTS:
        wts[name] = wts[name].astype(BF16)
    bp = x_prompt.shape[0]
    meta = jnp.broadcast_to(meta_tokens[None], (bp, N_META, D_MODEL))
    xp = jnp.concatenate([meta, x_prompt], axis=1)
    y_p, conv_p, ffn_p, k_p, v_p = _prompt_trunk(xp, wts)
    y_s, conv_s, ffn_s, k_s, v_s = _sample_trunk(x_sample, state_conv_a, state_ffn_conv,
                                                 cache_k, cache_v, page_table, wts)
    return (y_p, y_s, conv_p, ffn_p, k_p, v_p, conv_s, ffn_s, k_s, v_s)
```

```python
import functools

import jax
import jax.numpy as jnp
from jax import lax
from jax.experimental import pallas as pl
from jax.experimental.pallas import tpu as pltpu

D_MODEL = 2048
D_FF = 5632
N_HEADS = 16
HEAD_DIM = 128
N_META = 16
N_A_LAYERS = 2
DEPTH = 4
CONV_A_WIDTH = 31
FFN_CONV_WIDTH = 3
PAGE_SIZE = 128
EPS = 1e-6

LANES = 128
SUBLANES = 8
VMEM_LIMIT = 56 * 1024 * 1024

F32 = jnp.float32
BF16 = jnp.bfloat16


def _cparams(sem):
    return pltpu.CompilerParams(dimension_semantics=sem, vmem_limit_bytes=VMEM_LIMIT)


def _rms_rows(x, g):
    ms = jnp.mean(x * x, axis=-1, keepdims=True)
    return x * lax.rsqrt(ms + EPS) * g


def _head_rms(t, gh):
    parts = []
    for s in range(t.shape[1] // HEAD_DIM):
        parts.append(_rms_rows(t[:, s * HEAD_DIM:(s + 1) * HEAD_DIM], gh))
    return parts


LOG2E = 1.4426950408889634

Q_SCALE = HEAD_DIM ** -0.5 * LOG2E


def _softplus2(z2):
    return jnp.maximum(z2, 0.0) + jnp.log(1.0 + jnp.exp2(-jnp.abs(z2))) * LOG2E


def _split_bf16(x):
    hi = x.astype(BF16)
    lo = (x - hi.astype(F32)).astype(BF16)
    return hi, lo


def _glu_kernel(x_ref, g_ref, wa_ref, wg_ref, u_ref, h_scr):
    @pl.when(pl.program_id(1) == 0)
    def _():
        h_scr[...] = _rms_rows(x_ref[...], g_ref[...]).astype(BF16)

    h = h_scr[...]
    a = jnp.dot(h, wa_ref[...], preferred_element_type=F32)
    gt = jnp.dot(h, wg_ref[...], preferred_element_type=F32)
    u_ref[...] = a * jax.nn.sigmoid(gt)


def _glu(x, g, w_bf, tm, tn):
    m = x.shape[0]
    nj = D_MODEL // tn
    return pl.pallas_call(
        _glu_kernel,
        grid=(m // tm, nj),
        in_specs=[
            pl.BlockSpec((tm, D_MODEL), lambda i, j: (i, 0)),
            pl.BlockSpec((1, D_MODEL), lambda i, j: (0, 0)),
            pl.BlockSpec((D_MODEL, tn), lambda i, j: (0, j)),
            pl.BlockSpec((D_MODEL, tn), lambda i, j: (0, j + nj)),
        ],
        out_specs=pl.BlockSpec((tm, tn), lambda i, j: (i, j)),
        out_shape=jax.ShapeDtypeStruct((m, D_MODEL), F32),
        scratch_shapes=[pltpu.VMEM((tm, D_MODEL), BF16)],
        compiler_params=_cparams(("parallel", "arbitrary")),
        name="glu",
    )(x, g.reshape(1, D_MODEL), w_bf, w_bf)


DW_HALO = 32


def _dwconv_seq_kernel(u_ref, halo_ref, w_ref, b_ref, o_ref, full_scr, *, tt, cl):
    first = pl.program_id(1) == 0
    full_scr[0:DW_HALO, :] = jnp.where(first, 0.0, halo_ref[0])
    full_scr[DW_HALO:DW_HALO + tt, :] = u_ref[0]
    lead = DW_HALO - (CONV_A_WIDTH - 1)
    for c in range(cl // LANES):
        ls = slice(c * LANES, (c + 1) * LANES)
        acc = jnp.broadcast_to(b_ref[:, ls], (tt, LANES))
        for k in range(CONV_A_WIDTH):
            acc = acc + w_ref[k:k + 1, ls] * full_scr[lead + k:lead + k + tt, ls]
        o_ref[0, :, ls] = acc


def _dwconv_seq(u3, w, b, tt=128, cl=512):
    bsz, tp, _ = u3.shape
    w_pad = jnp.zeros((DW_HALO, D_MODEL), F32).at[:CONV_A_WIDTH].set(w)
    hb = tt // DW_HALO
    return pl.pallas_call(
        functools.partial(_dwconv_seq_kernel, tt=tt, cl=cl),
        grid=(bsz, tp // tt, D_MODEL // cl),
        in_specs=[
            pl.BlockSpec((1, tt, cl), lambda bb, i, c: (bb, i, c)),
            pl.BlockSpec((1, DW_HALO, cl), lambda bb, i, c: (bb, jnp.maximum(i * hb - 1, 0), c)),
            pl.BlockSpec((DW_HALO, cl), lambda bb, i, c: (0, c)),
            pl.BlockSpec((1, cl), lambda bb, i, c: (0, c)),
        ],
        out_specs=pl.BlockSpec((1, tt, cl), lambda bb, i, c: (bb, i, c)),
        out_shape=jax.ShapeDtypeStruct(u3.shape, F32),
        scratch_shapes=[pltpu.VMEM((DW_HALO + tt, cl), F32)],
        compiler_params=_cparams(("parallel", "parallel", "parallel")),
        name="dwconv_seq",
    )(u3, u3, w_pad, b.reshape(1, D_MODEL))


def _dwconv_step_kernel(s_ref, u_ref, w_ref, b_ref, o_ref):
    w = w_ref[...]
    past = jnp.sum(s_ref[...] * w[None, :CONV_A_WIDTH - 1, :], axis=1)
    o_ref[...] = past + u_ref[...] * w[CONV_A_WIDTH - 1:CONV_A_WIDTH, :] + b_ref[...]


def _dwconv_step(state, u, w, b):
    nb = state.shape[0]
    return pl.pallas_call(
        _dwconv_step_kernel,
        out_shape=jax.ShapeDtypeStruct((nb, D_MODEL), F32),
        compiler_params=pltpu.CompilerParams(vmem_limit_bytes=VMEM_LIMIT),
        name="dwconv_step",
    )(state, u, w, b.reshape(1, D_MODEL))


def _norm_silu_mm_res_kernel(c_ref, g_ref, w_ref, x_ref, o_ref, h_scr):
    @pl.when(pl.program_id(1) == 0)
    def _():
        y = _rms_rows(c_ref[...], g_ref[...])
        h_scr[...] = (y * jax.nn.sigmoid(y)).astype(BF16)

    o_ref[...] = x_ref[...] + jnp.dot(h_scr[...], w_ref[...], preferred_element_type=F32)


def _norm_silu_mm_res(c, g, w_bf, xres, tm, tn):
    m = c.shape[0]
    return pl.pallas_call(
        _norm_silu_mm_res_kernel,
        grid=(m // tm, D_MODEL // tn),
        in_specs=[
            pl.BlockSpec((tm, D_MODEL), lambda i, j: (i, 0)),
            pl.BlockSpec((1, D_MODEL), lambda i, j: (0, 0)),
            pl.BlockSpec((D_MODEL, tn), lambda i, j: (0, j)),
            pl.BlockSpec((tm, tn), lambda i, j: (i, j)),
        ],
        out_specs=pl.BlockSpec((tm, tn), lambda i, j: (i, j)),
        out_shape=jax.ShapeDtypeStruct((m, D_MODEL), F32),
        scratch_shapes=[pltpu.VMEM((tm, D_MODEL), BF16)],
        compiler_params=_cparams(("parallel", "arbitrary")),
        name="pw2",
    )(c, g.reshape(1, D_MODEL), w_bf, xres)


def _ffn_gate_seq_kernel(x_ref, xp_ref, g_ref, wg_ref, wu_ref, wc_ref, bc_ref,
                         act_ref, tail_ref, h_scr, hp_scr, gs_scr, *, tm, tiles_per_seq, tail_row):
    i = pl.program_id(0)

    @pl.when(pl.program_id(1) == 0)
    def _():
        h_scr[...] = _rms_rows(x_ref[...], g_ref[...]).astype(BF16)
        hp = _rms_rows(xp_ref[...], g_ref[...])
        hp = jnp.where(i % tiles_per_seq == 0, 0.0, hp)
        hp_scr[...] = jnp.concatenate([hp, hp], axis=0).astype(BF16)

    h = h_scr[...]
    wg = wg_ref[...]
    gate = jnp.dot(h, wg, preferred_element_type=F32)
    gprev = jnp.dot(hp_scr[...], wg, preferred_element_type=F32)
    up = jnp.dot(h, wu_ref[...], preferred_element_type=F32)
    gs_scr[0:SUBLANES, :] = gprev[0:SUBLANES]
    gs_scr[SUBLANES:SUBLANES + tm, :] = gate
    g1 = gs_scr[SUBLANES - 1:SUBLANES - 1 + tm, :]
    g2 = gs_scr[SUBLANES - 2:SUBLANES - 2 + tm, :]
    gc = wc_ref[0:1, :] * g2 + wc_ref[1:2, :] * g1 + wc_ref[2:3, :] * gate + bc_ref[...]
    act_ref[...] = (gc * jax.nn.sigmoid(gc) * up).astype(BF16)
    tail_ref[0] = gate[tail_row:tail_row + SUBLANES]


def _ffn_gate_seq(x, g, wg_bf, wu_bf, wc, bc, tm, tn, tiles_per_seq, tail_row):
    m = x.shape[0]
    nt = m // tm
    hb = tm // SUBLANES
    wc_pad = jnp.zeros((SUBLANES, D_FF), F32).at[:FFN_CONV_WIDTH].set(wc)
    return pl.pallas_call(
        functools.partial(_ffn_gate_seq_kernel, tm=tm, tiles_per_seq=tiles_per_seq, tail_row=tail_row),
        grid=(nt, D_FF // tn),
        in_specs=[
            pl.BlockSpec((tm, D_MODEL), lambda i, j: (i, 0)),
            pl.BlockSpec((SUBLANES, D_MODEL), lambda i, j: (jnp.maximum(i * hb - 1, 0), 0)),
            pl.BlockSpec((1, D_MODEL), lambda i, j: (0, 0)),
            pl.BlockSpec((D_MODEL, tn), lambda i, j: (0, j)),
            pl.BlockSpec((D_MODEL, tn), lambda i, j: (0, j)),
            pl.BlockSpec((SUBLANES, tn), lambda i, j: (0, j)),
            pl.BlockSpec((1, tn), lambda i, j: (0, j)),
        ],
        out_specs=[
            pl.BlockSpec((tm, tn), lambda i, j: (i, j)),
            pl.BlockSpec((1, SUBLANES, tn), lambda i, j: (i, 0, j)),
        ],
        out_shape=[
            jax.ShapeDtypeStruct((m, D_FF), BF16),
            jax.ShapeDtypeStruct((nt, SUBLANES, D_FF), F32),
        ],
        scratch_shapes=[
            pltpu.VMEM((tm, D_MODEL), BF16),
            pltpu.VMEM((2 * SUBLANES, D_MODEL), BF16),
            pltpu.VMEM((SUBLANES + tm, tn), F32),
        ],
        compiler_params=_cparams(("parallel", "arbitrary")),
        name="ffn_gate_seq",
    )(x, x, g.reshape(1, D_MODEL), wg_bf, wu_bf, wc_pad, bc.reshape(1, D_FF))


def _ffn_gate_step_kernel(x_ref, g_ref, wg_ref, wu_ref, wc_ref, bc_ref, s0_ref, s1_ref,
                          act_ref, gate_ref, h_scr):
    @pl.when(pl.program_id(0) == 0)
    def _():
        h_scr[...] = _rms_rows(x_ref[...], g_ref[...]).astype(BF16)

    h = h_scr[...]
    gate = jnp.dot(h, wg_ref[...], preferred_element_type=F32)
    up = jnp.dot(h, wu_ref[...], preferred_element_type=F32)
    gc = (wc_ref[0:1, :] * s0_ref[...] + wc_ref[1:2, :] * s1_ref[...]
          + wc_ref[2:3, :] * gate + bc_ref[...])
    act_ref[...] = (gc * jax.nn.sigmoid(gc) * up).astype(BF16)
    gate_ref[...] = gate


def _ffn_gate_step(x, g, wg_bf, wu_bf, wc, bc, s0, s1, tn):
    m = x.shape[0]
    wc_pad = jnp.zeros((SUBLANES, D_FF), F32).at[:FFN_CONV_WIDTH].set(wc)
    return pl.pallas_call(
        _ffn_gate_step_kernel,
        grid=(D_FF // tn,),
        in_specs=[
            pl.BlockSpec((m, D_MODEL), lambda j: (0, 0)),
            pl.BlockSpec((1, D_MODEL), lambda j: (0, 0)),
            pl.BlockSpec((D_MODEL, tn), lambda j: (0, j)),
            pl.BlockSpec((D_MODEL, tn), lambda j: (0, j)),
            pl.BlockSpec((SUBLANES, tn), lambda j: (0, j)),
            pl.BlockSpec((1, tn), lambda j: (0, j)),
            pl.BlockSpec((m, tn), lambda j: (0, j)),
            pl.BlockSpec((m, tn), lambda j: (0, j)),
        ],
        out_specs=[
            pl.BlockSpec((m, tn), lambda j: (0, j)),
            pl.BlockSpec((m, tn), lambda j: (0, j)),
        ],
        out_shape=[
            jax.ShapeDtypeStruct((m, D_FF), BF16),
            jax.ShapeDtypeStruct((m, D_FF), F32),
        ],
        scratch_shapes=[pltpu.VMEM((m, D_MODEL), BF16)],
        compiler_params=_cparams(("arbitrary",)),
        name="ffn_gate_step",
    )(x, g.reshape(1, D_MODEL), wg_bf, wu_bf, wc_pad, bc.reshape(1, D_FF), s0, s1)


def _mm_res_kernel(a_ref, w_ref, x_ref, o_ref):
    o_ref[...] = x_ref[...] + jnp.dot(a_ref[...], w_ref[...], preferred_element_type=F32)


def _mm_res(a_bf, w_bf, xres, tm, tn, name):
    m, kdim = a_bf.shape
    n = w_bf.shape[1]
    return pl.pallas_call(
        _mm_res_kernel,
        grid=(m // tm, n // tn),
        in_specs=[
            pl.BlockSpec((tm, kdim), lambda i, j: (i, 0)),
            pl.BlockSpec((kdim, tn), lambda i, j: (0, j)),
            pl.BlockSpec((tm, tn), lambda i, j: (i, j)),
        ],
        out_specs=pl.BlockSpec((tm, tn), lambda i, j: (i, j)),
        out_shape=jax.ShapeDtypeStruct((m, n), F32),
        compiler_params=_cparams(("parallel", "parallel")),
        name=name,
    )(a_bf, w_bf, xres)


def _kv_kernel(x_ref, g_ref, wk_ref, wv_ref, gk_ref, k_ref, v_ref, kb_ref, vb_ref, h_scr):
    @pl.when(pl.program_id(1) == 0)
    def _():
        h_scr[...] = _rms_rows(x_ref[...], g_ref[...]).astype(BF16)

    h = h_scr[...]
    kraw = jnp.dot(h, wk_ref[...], preferred_element_type=F32)
    v = jnp.dot(h, wv_ref[...], preferred_element_type=F32)
    for s, kn in enumerate(_head_rms(kraw, gk_ref[...])):
        k_ref[:, s * HEAD_DIM:(s + 1) * HEAD_DIM] = kn
        kb_ref[:, s * HEAD_DIM:(s + 1) * HEAD_DIM] = kn.astype(BF16)
    v_ref[...] = v
    vb_ref[...] = v.astype(BF16)


def _kv_proj(x, g, w_bf, gk, tm, tn):
    m = x.shape[0]
    nj = D_MODEL // tn
    blk = pl.BlockSpec((tm, tn), lambda i, j: (i, j))
    return pl.pallas_call(
        _kv_kernel,
        grid=(m // tm, nj),
        in_specs=[
            pl.BlockSpec((tm, D_MODEL), lambda i, j: (i, 0)),
            pl.BlockSpec((1, D_MODEL), lambda i, j: (0, 0)),
            pl.BlockSpec((D_MODEL, tn), lambda i, j: (0, j)),
            pl.BlockSpec((D_MODEL, tn), lambda i, j: (0, j + nj)),
            pl.BlockSpec((1, HEAD_DIM), lambda i, j: (0, 0)),
        ],
        out_specs=[blk, blk, blk, blk],
        out_shape=[
            jax.ShapeDtypeStruct((m, D_MODEL), F32),
            jax.ShapeDtypeStruct((m, D_MODEL), F32),
            jax.ShapeDtypeStruct((m, D_MODEL), BF16),
            jax.ShapeDtypeStruct((m, D_MODEL), BF16),
        ],
        scratch_shapes=[pltpu.VMEM((tm, D_MODEL), BF16)],
        compiler_params=_cparams(("parallel", "arbitrary")),
        name="kv_proj",
    )(x, g.reshape(1, D_MODEL), w_bf, w_bf, gk.reshape(1, HEAD_DIM))


def _q_kernel(x_ref, g_ref, w_ref, gq_ref, q_ref, h_scr):
    @pl.when(pl.program_id(1) == 0)
    def _():
        h_scr[...] = _rms_rows(x_ref[...], g_ref[...]).astype(BF16)

    qraw = jnp.dot(h_scr[...], w_ref[...], preferred_element_type=F32)
    for s, qn in enumerate(_head_rms(qraw, gq_ref[...])):
        q_ref[:, s * HEAD_DIM:(s + 1) * HEAD_DIM] = (qn * Q_SCALE).astype(BF16)


def _q_proj(x, g, w_bf, gq, tm, tn):
    m = x.shape[0]
    return pl.pallas_call(
        _q_kernel,
        grid=(m // tm, D_MODEL // tn),
        in_specs=[
            pl.BlockSpec((tm, D_MODEL), lambda i, j: (i, 0)),
            pl.BlockSpec((1, D_MODEL), lambda i, j: (0, 0)),
            pl.BlockSpec((D_MODEL, tn), lambda i, j: (0, j)),
            pl.BlockSpec((1, HEAD_DIM), lambda i, j: (0, 0)),
        ],
        out_specs=pl.BlockSpec((tm, tn), lambda i, j: (i, j)),
        out_shape=jax.ShapeDtypeStruct((m, D_MODEL), BF16),
        scratch_shapes=[pltpu.VMEM((tm, D_MODEL), BF16)],
        compiler_params=_cparams(("parallel", "arbitrary")),
        name="q_proj",
    )(x, g.reshape(1, D_MODEL), w_bf, gq.reshape(1, HEAD_DIM))


NT_DIMS = (((1,), (1,)), ((), ()))


def _neg_suffix(tk):
    r = lax.broadcasted_iota(jnp.int32, (tk, tk), 0)
    c = lax.broadcasted_iota(jnp.int32, (tk, tk), 1)
    return -(r >= c).astype(BF16)


def _neg_suffix_and_total(tk):
    return jnp.concatenate([_neg_suffix(tk), -jnp.ones((tk, tk), BF16)], axis=1)


def _sb_attn_seq_kernel(sb_ref, q_ref, k_ref, v_ref, nst_ref, o_ref, acc_scr, run_scr,
                        *, tq, tk, hs):
    hg = pl.program_id(1)
    qi = pl.program_id(2)
    sub = tq // tk
    nst = nst_ref[...]
    acc_scr[...] = jnp.zeros_like(acc_scr)
    run_scr[...] = jnp.zeros_like(run_scr)

    def block(j, r0, masked):
        rows = tq - r0
        off = pl.multiple_of(j * tk, tk)
        if masked:
            diff = (lax.broadcasted_iota(jnp.int32, (rows, tk), 0)
                    - lax.broadcasted_iota(jnp.int32, (rows, tk), 1))
            mask = diff > (j * tk - qi * tq - r0)
        lanes = [slice(s * HEAD_DIM, (s + 1) * HEAD_DIM) for s in range(hs)]
        zs = []
        for s in range(hs):
            ks = k_ref[0, pl.ds(off, tk), lanes[s]]
            zs.append(lax.dot_general(q_ref[0, r0:tq, lanes[s]], ks, NT_DIMS,
                                      preferred_element_type=F32) + sb_ref[hg * hs + s])
        css = []
        for s in range(hs):
            sp = _softplus2(zs[s])
            if masked:
                sp = jnp.where(mask, sp, 0.0)
            css.append(jnp.dot(jnp.concatenate(_split_bf16(sp), axis=1), nst,
                               preferred_element_type=F32))
        for s in range(hs):
            run = run_scr[s, r0:tq, :]
            att = jnp.exp2(zs[s] + css[s] + jnp.concatenate([run] * (tk // LANES), axis=1))
            if masked:
                att = jnp.where(mask, att, 0.0)
            vs = v_ref[0, pl.ds(off, tk), lanes[s]]
            acc_scr[s, r0:tq, :] += jnp.dot(att.astype(BF16), vs, preferred_element_type=F32)
            run_scr[s, r0:tq, :] = run + jnp.broadcast_to(css[s][:, 0:1], (rows, LANES))

    for d in reversed(range(sub)):
        block(qi * sub + d, d * tk, True)

    def body(jj, carry):
        block(qi * sub - 1 - jj, 0, False)
        return carry

    lax.fori_loop(0, qi * sub, body, 0)
    for s in range(hs):
        o_ref[0, :, s * HEAD_DIM:(s + 1) * HEAD_DIM] = acc_scr[s].astype(BF16)


def _sb_attn_seq(q3, kb3, vb3, sb2, tq=256, tk=256, hs=4):
    bsz, tp, _ = q3.shape
    wl = hs * HEAD_DIM
    grid_spec = pltpu.PrefetchScalarGridSpec(
        num_scalar_prefetch=1,
        grid=(bsz, N_HEADS // hs, tp // tq),
        in_specs=[
            pl.BlockSpec((1, tq, wl), lambda b, h, i, sbr: (b, i, h)),
            pl.BlockSpec((1, tp, wl), lambda b, h, i, sbr: (b, 0, h)),
            pl.BlockSpec((1, tp, wl), lambda b, h, i, sbr: (b, 0, h)),
            pl.BlockSpec((2 * tk, tk), lambda b, h, i, sbr: (0, 0)),
        ],
        out_specs=pl.BlockSpec((1, tq, wl), lambda b, h, i, sbr: (b, i, h)),
        scratch_shapes=[pltpu.VMEM((hs, tq, HEAD_DIM), F32), pltpu.VMEM((hs, tq, LANES), F32)],
    )
    return pl.pallas_call(
        functools.partial(_sb_attn_seq_kernel, tq=tq, tk=tk, hs=hs),
        grid_spec=grid_spec,
        out_shape=jax.ShapeDtypeStruct(q3.shape, BF16),
        compiler_params=_cparams(("parallel", "parallel", "arbitrary")),
        name="sb_attn_seq",
    )(sb2, q3, kb3, vb3, jnp.concatenate([_neg_suffix(tk)] * 2, axis=0))


PAGE_ROWS = PAGE_SIZE * N_HEADS
PAGES_PER_STEP = 4


def _sb_attn_paged_kernel(pt_ref, q_ref, bias_ref, *refs, n_steps, pp):
    k_refs, v_refs = refs[:pp], refs[pp:2 * pp]
    nst_ref, o_ref, acc_scr, run_scr = refs[2 * pp:]
    step = pl.program_id(1)

    @pl.when(step == 0)
    def _():
        acc_scr[...] = jnp.zeros_like(acc_scr)
        run_scr[...] = jnp.zeros_like(run_scr)

    n_tiles = PAGE_ROWS // LANES
    q = q_ref[0]
    bias = jnp.concatenate([bias_ref[...]] * n_tiles, axis=1)
    own = ((lax.broadcasted_iota(jnp.int32, (N_HEADS, PAGE_ROWS), 1) & (N_HEADS - 1))
           == lax.broadcasted_iota(jnp.int32, (N_HEADS, PAGE_ROWS), 0))
    nst = nst_ref[...]
    run = run_scr[...]
    acc = acc_scr[...]
    for i in range(pp):
        kb = k_refs[i][0].astype(BF16)
        z = lax.dot_general(q, kb, NT_DIMS, preferred_element_type=F32) + bias
        sp = jnp.where(own, _softplus2(z), 0.0)
        sp_st = jnp.concatenate([sp[:, t * LANES:(t + 1) * LANES] for t in range(n_tiles)], axis=0)
        cs = jnp.dot(jnp.concatenate(_split_bf16(sp_st), axis=1), nst,
                     preferred_element_type=F32)
        logw = [None] * n_tiles
        for t in reversed(range(n_tiles)):
            rows = slice(t * N_HEADS, (t + 1) * N_HEADS)
            logw[t] = z[:, t * LANES:(t + 1) * LANES] + cs[rows, :LANES] + run
            run = run + cs[rows, LANES:]
        att = jnp.where(own, jnp.exp2(jnp.concatenate(logw, axis=1)), 0.0)
        vb = v_refs[i][0].astype(BF16)
        acc = acc + jnp.dot(att.astype(BF16), vb, preferred_element_type=F32)
    acc_scr[...] = acc
    run_scr[...] = run

    @pl.when(step == n_steps - 1)
    def _():
        o_ref[0] = acc


def _sb_attn_paged(q, cache_k, cache_v, page_table, sb2):
    assert N_HEADS & (N_HEADS - 1) == 0 and LANES % N_HEADS == 0
    nb, n_pages = page_table.shape
    n_pool = cache_k.shape[0]
    pp = PAGES_PER_STEP
    n_steps = n_pages // pp
    assert n_steps * pp == n_pages
    ck = cache_k.reshape(n_pool, PAGE_ROWS, HEAD_DIM)
    cv = cache_v.reshape(n_pool, PAGE_ROWS, HEAD_DIM)
    q3 = q.reshape(nb, N_HEADS, HEAD_DIM)
    bias = jnp.broadcast_to(sb2.astype(F32)[:, None], (N_HEADS, LANES))

    def page_spec(i):
        return pl.BlockSpec((1, PAGE_ROWS, HEAD_DIM),
                            lambda b, s, pt: (pt[b, n_pages - 1 - (s * pp + i)], 0, 0))

    grid_spec = pltpu.PrefetchScalarGridSpec(
        num_scalar_prefetch=1,
        grid=(nb, n_steps),
        in_specs=[
            pl.BlockSpec((1, N_HEADS, HEAD_DIM), lambda b, s, pt: (b, 0, 0)),
            pl.BlockSpec((N_HEADS, LANES), lambda b, s, pt: (0, 0)),
            *[page_spec(i) for i in range(pp)],
            *[page_spec(i) for i in range(pp)],
            pl.BlockSpec((2 * LANES, 2 * LANES), lambda b, s, pt: (0, 0)),
        ],
        out_specs=pl.BlockSpec((1, N_HEADS, HEAD_DIM), lambda b, s, pt: (b, 0, 0)),
        scratch_shapes=[pltpu.VMEM((N_HEADS, HEAD_DIM), F32), pltpu.VMEM((N_HEADS, LANES), F32)],
    )
    o = pl.pallas_call(
        functools.partial(_sb_attn_paged_kernel, n_steps=n_steps, pp=pp),
        grid_spec=grid_spec,
        out_shape=jax.ShapeDtypeStruct((nb, N_HEADS, HEAD_DIM), F32),
        compiler_params=_cparams(("parallel", "arbitrary")),
        name="sb_attn_paged",
    )(page_table, q3, bias, *([ck] * pp), *([cv] * pp),
      jnp.concatenate([_neg_suffix_and_total(LANES)] * 2, axis=0))
    return o.reshape(nb, D_MODEL)


T_PAD = 4352
TM_P = 1088
TM_DOWN = 544
TN_NORM = 256
TN_MM = 512
SAMPLE_ROWS = 16


def _prompt_trunk(x, wts):
    bsz, t_real, _ = x.shape
    pad = T_PAD - t_real
    x = jnp.pad(x, ((0, 0), (0, pad), (0, 0))).reshape(bsz * T_PAD, D_MODEL)
    tiles_per_seq = T_PAD // TM_P
    tail_row = (t_real - 1) % TM_P - (SUBLANES - 1)
    assert (t_real - 1) // TM_P == tiles_per_seq - 1 and tail_row % SUBLANES == 0
    conv_states, ffn_states = [], []
    k = v = kb = vb = None
    for l in range(DEPTH):
        if l < N_A_LAYERS:
            u = _glu(x, wts["a_g_norm"][l], wts["a_w_pw1"][l], TM_P, TN_NORM)
            u3 = u.reshape(bsz, T_PAD, D_MODEL)
            conv_states.append(u3[:, t_real - (CONV_A_WIDTH - 1):t_real])
            c = _dwconv_seq(u3, wts["a_w_dw"][l], wts["a_b_dw"][l])
            x = _norm_silu_mm_res(c.reshape(bsz * T_PAD, D_MODEL), wts["a_g_cn"][l],
                                  wts["a_w_pw2"][l], x, TM_P, TN_NORM)
        else:
            j = l - N_A_LAYERS
            q = _q_proj(x, wts["b_g_norm"][j], wts["b_w_q"][j], wts["b_g_q"][j], TM_P, TN_NORM)
            o = _sb_attn_seq(q.reshape(bsz, T_PAD, D_MODEL), kb, vb, wts["b_sb"][j] * LOG2E)
            x = _mm_res(o.reshape(bsz * T_PAD, D_MODEL), wts["b_w_o"][j], x, TM_P, TN_MM, "o_proj")
        act, tail = _ffn_gate_seq(x, wts["f_g_norm"][l], wts["f_w_gate"][l], wts["f_w_up"][l],
                                  wts["f_w_conv"][l], wts["f_b_conv"][l], TM_P, TN_NORM,
                                  tiles_per_seq, tail_row)
        tail = tail.reshape(bsz, tiles_per_seq, SUBLANES, D_FF)
        ffn_states.append(tail[:, tiles_per_seq - 1, SUBLANES - (FFN_CONV_WIDTH - 1):])
        x = _mm_res(act, wts["f_w_down"][l], x, TM_DOWN, TN_MM, "ffn_down")
        if l == N_A_LAYERS - 1:
            k, v, kb, vb = _kv_proj(x, wts["kv_g_norm"], wts["w_kv"], wts["g_k"], TM_P, TN_NORM)
            kb = kb.reshape(bsz, T_PAD, D_MODEL)
            vb = vb.reshape(bsz, T_PAD, D_MODEL)
    x = x.reshape(bsz, T_PAD, D_MODEL)[:, N_META:t_real]
    k = k.reshape(bsz, T_PAD, N_HEADS, HEAD_DIM)[:, :t_real]
    v = v.reshape(bsz, T_PAD, N_HEADS, HEAD_DIM)[:, :t_real]
    return x, jnp.stack(conv_states), jnp.stack(ffn_states), k, v


def _sample_trunk(x, state_conv_a, state_ffn, cache_k, cache_v, page_table, wts):
    nb = x.shape[0]
    m = SAMPLE_ROWS

    def pad_rows(a):
        return jnp.pad(a, ((0, m - nb), (0, 0)))

    x = pad_rows(x.reshape(nb, D_MODEL))
    conv_states, ffn_states = [], []
    k = v = None
    for l in range(DEPTH):
        if l < N_A_LAYERS:
            u = _glu(x, wts["a_g_norm"][l], wts["a_w_pw1"][l], m, TN_MM)[:nb]
            past = state_conv_a[l]
            conv_states.append(jnp.concatenate([past[:, 1:], u[:, None]], axis=1))
            c = _dwconv_step(past, u, wts["a_w_dw"][l], wts["a_b_dw"][l])
            x = _norm_silu_mm_res(pad_rows(c), wts["a_g_cn"][l], wts["a_w_pw2"][l], x, m, TN_MM)
        else:
            j = l - N_A_LAYERS
            q = _q_proj(x, wts["b_g_norm"][j], wts["b_w_q"][j], wts["b_g_q"][j], m, TN_MM)
            o = _sb_attn_paged(q[:nb], cache_k, cache_v, page_table, wts["b_sb"][j] * LOG2E)
            x = _mm_res(pad_rows(o).astype(BF16), wts["b_w_o"][j], x, m, TN_MM, "o_proj")
        past = state_ffn[l]
        act, gate = _ffn_gate_step(x, wts["f_g_norm"][l], wts["f_w_gate"][l], wts["f_w_up"][l],
                                   wts["f_w_conv"][l], wts["f_b_conv"][l],
                                   pad_rows(past[:, 0]), pad_rows(past[:, 1]), TN_MM)
        ffn_states.append(jnp.stack([past[:, 1], gate[:nb]], axis=1))
        x = _mm_res(act, wts["f_w_down"][l], x, m, TN_MM, "ffn_down")
        if l == N_A_LAYERS - 1:
            k, v, _, _ = _kv_proj(x, wts["kv_g_norm"], wts["w_kv"], wts["g_k"], m, TN_MM)
    y = x[:nb].reshape(nb, 1, D_MODEL)
    k = k[:nb].reshape(nb, 1, N_HEADS, HEAD_DIM)
    v = v[:nb].reshape(nb, 1, N_HEADS, HEAD_DIM)
    return y, jnp.stack(conv_states), jnp.stack(ffn_states), k, v


_MATMUL_WEIGHTS = ("a_w_pw1", "a_w_pw2", "f_w_gate", "f_w_up", "f_w_down", "w_kv", "b_w_q", "b_w_o")


def kernel(x_prompt, x_sample, state_conv_a, state_ffn_conv, cache_k, cache_v, page_table, meta_tokens, a_g_norm, a_w_pw1, a_w_dw, a_b_dw, a_g_cn, a_w_pw2, f_g_norm, f_w_gate, f_w_up, f_w_down, f_w_conv, f_b_conv, kv_g_norm, w_kv, g_k, b_g_norm, b_w_q, b_g_q, b_sb, b_w_o):
    wts = dict(a_g_norm=a_g_norm, a_w_pw1=a_w_pw1, a_w_dw=a_w_dw, a_b_dw=a_b_dw, a_g_cn=a_g_cn,
               a_w_pw2=a_w_pw2, f_g_norm=f_g_norm, f_w_gate=f_w_gate, f_w_up=f_w_up,
               f_w_down=f_w_down, f_w_conv=f_w_conv, f_b_conv=f_b_conv, kv_g_norm=kv_g_norm,
               w_kv=w_kv, g_k=g_k, b_g_norm=b_g_norm, b_w_q=b_w_q, b_g_q=b_g_q, b_sb=b_sb,
               b_w_o=b_w_o)
    for name in _MATMUL_WEIGHTS:
        wts[name] = wts[name].astype(BF16)
    bp = x_prompt.shape[0]
    meta = jnp.broadcast_to(meta_tokens[None], (bp, N_META, D_MODEL))
    xp = jnp.concatenate([meta, x_prompt], axis=1)
    y_p, conv_p, ffn_p, k_p, v_p = _prompt_trunk(xp, wts)
    y_s, conv_s, ffn_s, k_s, v_s = _sample_trunk(x_sample, state_conv_a, state_ffn_conv,
                                                 cache_k, cache_v, page_table, wts)
    return (y_p, y_s, conv_p, ffn_p, k_p, v_p, conv_s, ffn_s, k_s, v_s)
```

```python
import functools

import jax
import jax.numpy as jnp
from jax import lax
from jax.experimental import pallas as pl
from jax.experimental.pallas import tpu as pltpu

D_MODEL = 2048
D_FF = 5632
N_HEADS = 16
HEAD_DIM = 128
N_META = 16
N_A_LAYERS = 2
DEPTH = 4
CONV_A_WIDTH = 31
FFN_CONV_WIDTH = 3
PAGE_SIZE = 128
EPS = 1e-6

LANES = 128
SUBLANES = 8
VMEM_LIMIT = 56 * 1024 * 1024

F32 = jnp.float32
BF16 = jnp.bfloat16


def _cparams(sem):
    return pltpu.CompilerParams(dimension_semantics=sem, vmem_limit_bytes=VMEM_LIMIT)


def _rms_rows(x, g):
    ms = jnp.mean(x * x, axis=-1, keepdims=True)
    return x * lax.rsqrt(ms + EPS) * g


def _head_rms(t, gh):
    parts = []
    for s in range(t.shape[1] // HEAD_DIM):
        parts.append(_rms_rows(t[:, s * HEAD_DIM:(s + 1) * HEAD_DIM], gh))
    return parts


LOG2E = 1.4426950408889634

Q_SCALE = HEAD_DIM ** -0.5 * LOG2E


def _softplus2(z2):
    return jnp.maximum(z2, 0.0) + jnp.log(1.0 + jnp.exp2(-jnp.abs(z2))) * LOG2E


ROW_CHUNK = 272


def _row_chunks(tm):
    rc = ROW_CHUNK if tm % ROW_CHUNK == 0 else tm
    return [slice(r0, r0 + rc) for r0 in range(0, tm, rc)]


def _glu_kernel(x_ref, g_ref, wa_ref, wg_ref, u_ref, h_scr):
    @pl.when(pl.program_id(1) == 0)
    def _():
        h_scr[...] = _rms_rows(x_ref[...], g_ref[...]).astype(BF16)

    for rows in _row_chunks(h_scr.shape[0]):
        h = h_scr[rows, :]
        a = jnp.dot(h, wa_ref[...], preferred_element_type=F32)
        gt = jnp.dot(h, wg_ref[...], preferred_element_type=F32)
        u_ref[rows, :] = a * jax.nn.sigmoid(gt)


def _glu(x, g, w_bf, tm, tn):
    m = x.shape[0]
    nj = D_MODEL // tn
    return pl.pallas_call(
        _glu_kernel,
        grid=(m // tm, nj),
        in_specs=[
            pl.BlockSpec((tm, D_MODEL), lambda i, j: (i, 0)),
            pl.BlockSpec((1, D_MODEL), lambda i, j: (0, 0)),
            pl.BlockSpec((D_MODEL, tn), lambda i, j: (0, j)),
            pl.BlockSpec((D_MODEL, tn), lambda i, j: (0, j + nj)),
        ],
        out_specs=pl.BlockSpec((tm, tn), lambda i, j: (i, j)),
        out_shape=jax.ShapeDtypeStruct((m, D_MODEL), F32),
        scratch_shapes=[pltpu.VMEM((tm, D_MODEL), BF16)],
        compiler_params=_cparams(("parallel", "arbitrary")),
        name="glu",
    )(x, g.reshape(1, D_MODEL), w_bf, w_bf)


DW_HALO = 32


def _dwconv_seq_kernel(u_ref, halo_ref, w_ref, b_ref, o_ref, full_scr, sh_scr, *, tt, cl):
    first = pl.program_id(1) == 0
    full_scr[0:DW_HALO, :] = jnp.where(first, 0.0, halo_ref[0])
    full_scr[DW_HALO:DW_HALO + tt, :] = u_ref[0]
    lead = DW_HALO - (CONV_A_WIDTH - 1)
    sh_rows = sh_scr.shape[1]
    for c in range(cl // LANES):
        ls = slice(c * LANES, (c + 1) * LANES)
        for s in range(1, SUBLANES):
            sh_scr[s - 1, :, ls] = full_scr[s:s + sh_rows, ls]
        acc = jnp.broadcast_to(b_ref[:, ls], (tt, LANES))
        for k in range(CONV_A_WIDTH):
            s, a0 = (lead + k) % SUBLANES, (lead + k) // SUBLANES * SUBLANES
            rows = full_scr[a0:a0 + tt, ls] if s == 0 else sh_scr[s - 1, a0:a0 + tt, ls]
            acc = acc + w_ref[k:k + 1, ls] * rows
        o_ref[0, :, ls] = acc


def _dwconv_seq(u3, w, b, tt=128, cl=512):
    bsz, tp, _ = u3.shape
    w_pad = jnp.zeros((DW_HALO, D_MODEL), F32).at[:CONV_A_WIDTH].set(w)
    hb = tt // DW_HALO
    return pl.pallas_call(
        functools.partial(_dwconv_seq_kernel, tt=tt, cl=cl),
        grid=(bsz, tp // tt, D_MODEL // cl),
        in_specs=[
            pl.BlockSpec((1, tt, cl), lambda bb, i, c: (bb, i, c)),
            pl.BlockSpec((1, DW_HALO, cl), lambda bb, i, c: (bb, jnp.maximum(i * hb - 1, 0), c)),
            pl.BlockSpec((DW_HALO, cl), lambda bb, i, c: (0, c)),
            pl.BlockSpec((1, cl), lambda bb, i, c: (0, c)),
        ],
        out_specs=pl.BlockSpec((1, tt, cl), lambda bb, i, c: (bb, i, c)),
        out_shape=jax.ShapeDtypeStruct(u3.shape, F32),
        scratch_shapes=[pltpu.VMEM((DW_HALO + tt, cl), F32),
                        pltpu.VMEM((SUBLANES - 1, DW_HALO + tt - SUBLANES, cl), F32)],
        compiler_params=_cparams(("parallel", "parallel", "parallel")),
        name="dwconv_seq",
    )(u3, u3, w_pad, b.reshape(1, D_MODEL))


def _dwconv_step_kernel(s_ref, u_ref, w_ref, b_ref, o_ref):
    w = w_ref[...]
    past = jnp.sum(s_ref[...] * w[None, :CONV_A_WIDTH - 1, :], axis=1)
    o_ref[...] = past + u_ref[...] * w[CONV_A_WIDTH - 1:CONV_A_WIDTH, :] + b_ref[...]


def _dwconv_step(state, u, w, b):
    nb = state.shape[0]
    return pl.pallas_call(
        _dwconv_step_kernel,
        out_shape=jax.ShapeDtypeStruct((nb, D_MODEL), F32),
        compiler_params=pltpu.CompilerParams(vmem_limit_bytes=VMEM_LIMIT),
        name="dwconv_step",
    )(state, u, w, b.reshape(1, D_MODEL))


def _norm_silu_mm_res_kernel(c_ref, g_ref, w_ref, x_ref, o_ref, h_scr):
    @pl.when(pl.program_id(1) == 0)
    def _():
        y = _rms_rows(c_ref[...], g_ref[...])
        h_scr[...] = (y * jax.nn.sigmoid(y)).astype(BF16)

    for rows in _row_chunks(h_scr.shape[0]):
        o_ref[rows, :] = x_ref[rows, :] + jnp.dot(h_scr[rows, :], w_ref[...],
                                                  preferred_element_type=F32)


def _norm_silu_mm_res(c, g, w_bf, xres, tm, tn):
    m = c.shape[0]
    return pl.pallas_call(
        _norm_silu_mm_res_kernel,
        grid=(m // tm, D_MODEL // tn),
        in_specs=[
            pl.BlockSpec((tm, D_MODEL), lambda i, j: (i, 0)),
            pl.BlockSpec((1, D_MODEL), lambda i, j: (0, 0)),
            pl.BlockSpec((D_MODEL, tn), lambda i, j: (0, j)),
            pl.BlockSpec((tm, tn), lambda i, j: (i, j)),
        ],
        out_specs=pl.BlockSpec((tm, tn), lambda i, j: (i, j)),
        out_shape=jax.ShapeDtypeStruct((m, D_MODEL), F32),
        scratch_shapes=[pltpu.VMEM((tm, D_MODEL), BF16)],
        compiler_params=_cparams(("parallel", "arbitrary")),
        name="pw2",
    )(c, g.reshape(1, D_MODEL), w_bf, xres)


def _ffn_gate_seq_kernel(x_ref, xp_ref, g_ref, wg_ref, wu_ref, wc_ref, bc_ref,
                         act_ref, tail_ref, h_scr, hp_scr, gs_scr, *, tm, tiles_per_seq, tail_row):
    i = pl.program_id(0)

    @pl.when(pl.program_id(1) == 0)
    def _():
        h_scr[...] = _rms_rows(x_ref[...], g_ref[...]).astype(BF16)
        hp = _rms_rows(xp_ref[...], g_ref[...])
        hp = jnp.where(i % tiles_per_seq == 0, 0.0, hp)
        hp_scr[...] = jnp.concatenate([hp, hp], axis=0).astype(BF16)

    wg = wg_ref[...]
    gprev = jnp.dot(hp_scr[...], wg, preferred_element_type=F32)
    gs_scr[0:SUBLANES, :] = gprev[0:SUBLANES]
    for rows in _row_chunks(tm):
        r0, r1 = rows.start, rows.stop
        h = h_scr[rows, :]
        gate = jnp.dot(h, wg, preferred_element_type=F32)
        up = jnp.dot(h, wu_ref[...], preferred_element_type=F32)
        gs_scr[SUBLANES + r0:SUBLANES + r1, :] = gate
        g1 = gs_scr[SUBLANES - 1 + r0:SUBLANES - 1 + r1, :]
        g2 = gs_scr[SUBLANES - 2 + r0:SUBLANES - 2 + r1, :]
        gc = wc_ref[0:1, :] * g2 + wc_ref[1:2, :] * g1 + wc_ref[2:3, :] * gate + bc_ref[...]
        act_ref[rows, :] = (gc * jax.nn.sigmoid(gc) * up).astype(BF16)
        if r0 <= tail_row < r1:
            tail_ref[0] = gate[tail_row - r0:tail_row - r0 + SUBLANES]


def _ffn_gate_seq(x, g, wg_bf, wu_bf, wc, bc, tm, tn, tiles_per_seq, tail_row):
    m = x.shape[0]
    nt = m // tm
    hb = tm // SUBLANES
    wc_pad = jnp.zeros((SUBLANES, D_FF), F32).at[:FFN_CONV_WIDTH].set(wc)
    return pl.pallas_call(
        functools.partial(_ffn_gate_seq_kernel, tm=tm, tiles_per_seq=tiles_per_seq, tail_row=tail_row),
        grid=(nt, D_FF // tn),
        in_specs=[
            pl.BlockSpec((tm, D_MODEL), lambda i, j: (i, 0)),
            pl.BlockSpec((SUBLANES, D_MODEL), lambda i, j: (jnp.maximum(i * hb - 1, 0), 0)),
            pl.BlockSpec((1, D_MODEL), lambda i, j: (0, 0)),
            pl.BlockSpec((D_MODEL, tn), lambda i, j: (0, j)),
            pl.BlockSpec((D_MODEL, tn), lambda i, j: (0, j)),
            pl.BlockSpec((SUBLANES, tn), lambda i, j: (0, j)),
            pl.BlockSpec((1, tn), lambda i, j: (0, j)),
        ],
        out_specs=[
            pl.BlockSpec((tm, tn), lambda i, j: (i, j)),
            pl.BlockSpec((1, SUBLANES, tn), lambda i, j: (i, 0, j)),
        ],
        out_shape=[
            jax.ShapeDtypeStruct((m, D_FF), BF16),
            jax.ShapeDtypeStruct((nt, SUBLANES, D_FF), F32),
        ],
        scratch_shapes=[
            pltpu.VMEM((tm, D_MODEL), BF16),
            pltpu.VMEM((2 * SUBLANES, D_MODEL), BF16),
            pltpu.VMEM((SUBLANES + tm, tn), F32),
        ],
        compiler_params=_cparams(("parallel", "arbitrary")),
        name="ffn_gate_seq",
    )(x, x, g.reshape(1, D_MODEL), wg_bf, wu_bf, wc_pad, bc.reshape(1, D_FF))


def _ffn_gate_step_kernel(x_ref, g_ref, wg_ref, wu_ref, wc_ref, bc_ref, s0_ref, s1_ref,
                          act_ref, gate_ref, h_scr):
    @pl.when(pl.program_id(0) == 0)
    def _():
        h_scr[...] = _rms_rows(x_ref[...], g_ref[...]).astype(BF16)

    h = h_scr[...]
    gate = jnp.dot(h, wg_ref[...], preferred_element_type=F32)
    up = jnp.dot(h, wu_ref[...], preferred_element_type=F32)
    gc = (wc_ref[0:1, :] * s0_ref[...] + wc_ref[1:2, :] * s1_ref[...]
          + wc_ref[2:3, :] * gate + bc_ref[...])
    act_ref[...] = (gc * jax.nn.sigmoid(gc) * up).astype(BF16)
    gate_ref[...] = gate


def _ffn_gate_step(x, g, wg_bf, wu_bf, wc, bc, s0, s1, tn):
    m = x.shape[0]
    wc_pad = jnp.zeros((SUBLANES, D_FF), F32).at[:FFN_CONV_WIDTH].set(wc)
    return pl.pallas_call(
        _ffn_gate_step_kernel,
        grid=(D_FF // tn,),
        in_specs=[
            pl.BlockSpec((m, D_MODEL), lambda j: (0, 0)),
            pl.BlockSpec((1, D_MODEL), lambda j: (0, 0)),
            pl.BlockSpec((D_MODEL, tn), lambda j: (0, j)),
            pl.BlockSpec((D_MODEL, tn), lambda j: (0, j)),
            pl.BlockSpec((SUBLANES, tn), lambda j: (0, j)),
            pl.BlockSpec((1, tn), lambda j: (0, j)),
            pl.BlockSpec((m, tn), lambda j: (0, j)),
            pl.BlockSpec((m, tn), lambda j: (0, j)),
        ],
        out_specs=[
            pl.BlockSpec((m, tn), lambda j: (0, j)),
            pl.BlockSpec((m, tn), lambda j: (0, j)),
        ],
        out_shape=[
            jax.ShapeDtypeStruct((m, D_FF), BF16),
            jax.ShapeDtypeStruct((m, D_FF), F32),
        ],
        scratch_shapes=[pltpu.VMEM((m, D_MODEL), BF16)],
        compiler_params=_cparams(("arbitrary",)),
        name="ffn_gate_step",
    )(x, g.reshape(1, D_MODEL), wg_bf, wu_bf, wc_pad, bc.reshape(1, D_FF), s0, s1)


def _mm_res_kernel(a_ref, w_ref, x_ref, o_ref):
    for rows in _row_chunks(a_ref.shape[0]):
        o_ref[rows, :] = x_ref[rows, :] + jnp.dot(a_ref[rows, :], w_ref[...],
                                                  preferred_element_type=F32)


def _mm_res(a_bf, w_bf, xres, tm, tn, name):
    m, kdim = a_bf.shape
    n = w_bf.shape[1]
    return pl.pallas_call(
        _mm_res_kernel,
        grid=(m // tm, n // tn),
        in_specs=[
            pl.BlockSpec((tm, kdim), lambda i, j: (i, 0)),
            pl.BlockSpec((kdim, tn), lambda i, j: (0, j)),
            pl.BlockSpec((tm, tn), lambda i, j: (i, j)),
        ],
        out_specs=pl.BlockSpec((tm, tn), lambda i, j: (i, j)),
        out_shape=jax.ShapeDtypeStruct((m, n), F32),
        compiler_params=_cparams(("parallel", "parallel")),
        name=name,
    )(a_bf, w_bf, xres)


def _kv_kernel(x_ref, g_ref, wk_ref, wv_ref, gk_ref, k_ref, v_ref, kb_ref, vb_ref, h_scr):
    @pl.when(pl.program_id(1) == 0)
    def _():
        h_scr[...] = _rms_rows(x_ref[...], g_ref[...]).astype(BF16)

    for rows in _row_chunks(h_scr.shape[0]):
        h = h_scr[rows, :]
        kraw = jnp.dot(h, wk_ref[...], preferred_element_type=F32)
        v = jnp.dot(h, wv_ref[...], preferred_element_type=F32)
        for s, kn in enumerate(_head_rms(kraw, gk_ref[...])):
            k_ref[rows, s * HEAD_DIM:(s + 1) * HEAD_DIM] = kn
            kb_ref[rows, s * HEAD_DIM:(s + 1) * HEAD_DIM] = kn.astype(BF16)
        v_ref[rows, :] = v
        vb_ref[rows, :] = v.astype(BF16)


def _kv_proj(x, g, w_bf, gk, tm, tn):
    m = x.shape[0]
    nj = D_MODEL // tn
    blk = pl.BlockSpec((tm, tn), lambda i, j: (i, j))
    return pl.pallas_call(
        _kv_kernel,
        grid=(m // tm, nj),
        in_specs=[
            pl.BlockSpec((tm, D_MODEL), lambda i, j: (i, 0)),
            pl.BlockSpec((1, D_MODEL), lambda i, j: (0, 0)),
            pl.BlockSpec((D_MODEL, tn), lambda i, j: (0, j)),
            pl.BlockSpec((D_MODEL, tn), lambda i, j: (0, j + nj)),
            pl.BlockSpec((1, HEAD_DIM), lambda i, j: (0, 0)),
        ],
        out_specs=[blk, blk, blk, blk],
        out_shape=[
            jax.ShapeDtypeStruct((m, D_MODEL), F32),
            jax.ShapeDtypeStruct((m, D_MODEL), F32),
            jax.ShapeDtypeStruct((m, D_MODEL), BF16),
            jax.ShapeDtypeStruct((m, D_MODEL), BF16),
        ],
        scratch_shapes=[pltpu.VMEM((tm, D_MODEL), BF16)],
        compiler_params=_cparams(("parallel", "arbitrary")),
        name="kv_proj",
    )(x, g.reshape(1, D_MODEL), w_bf, w_bf, gk.reshape(1, HEAD_DIM))


def _q_kernel(x_ref, g_ref, w_ref, gq_ref, q_ref, h_scr):
    @pl.when(pl.program_id(1) == 0)
    def _():
        h_scr[...] = _rms_rows(x_ref[...], g_ref[...]).astype(BF16)

    for rows in _row_chunks(h_scr.shape[0]):
        qraw = jnp.dot(h_scr[rows, :], w_ref[...], preferred_element_type=F32)
        for s, qn in enumerate(_head_rms(qraw, gq_ref[...])):
            q_ref[rows, s * HEAD_DIM:(s + 1) * HEAD_DIM] = (qn * Q_SCALE).astype(BF16)


def _q_proj(x, g, w_bf, gq, tm, tn):
    m = x.shape[0]
    return pl.pallas_call(
        _q_kernel,
        grid=(m // tm, D_MODEL // tn),
        in_specs=[
            pl.BlockSpec((tm, D_MODEL), lambda i, j: (i, 0)),
            pl.BlockSpec((1, D_MODEL), lambda i, j: (0, 0)),
            pl.BlockSpec((D_MODEL, tn), lambda i, j: (0, j)),
            pl.BlockSpec((1, HEAD_DIM), lambda i, j: (0, 0)),
        ],
        out_specs=pl.BlockSpec((tm, tn), lambda i, j: (i, j)),
        out_shape=jax.ShapeDtypeStruct((m, D_MODEL), BF16),
        scratch_shapes=[pltpu.VMEM((tm, D_MODEL), BF16)],
        compiler_params=_cparams(("parallel", "arbitrary")),
        name="q_proj",
    )(x, g.reshape(1, D_MODEL), w_bf, gq.reshape(1, HEAD_DIM))


NT_DIMS = (((1,), (1,)), ((), ()))


def _neg_suffix(tk):
    r = lax.broadcasted_iota(jnp.int32, (tk, tk), 0)
    c = lax.broadcasted_iota(jnp.int32, (tk, tk), 1)
    return -(r >= c).astype(BF16)


def _neg_suffix_and_total(tk):
    return jnp.concatenate([_neg_suffix(tk), -jnp.ones((tk, tk), BF16)], axis=1)


def _sb_attn_seq_kernel(sb_ref, q_ref, k_ref, v_ref, nst_ref, o_ref, acc_scr, run_scr,
                        *, tq, tk, hs):
    hg = pl.program_id(1)
    qi = pl.program_id(2)
    sub = tq // tk
    nst = nst_ref[...]
    acc_scr[...] = jnp.zeros_like(acc_scr)
    run_scr[...] = jnp.zeros_like(run_scr)

    def block(j, r0, masked):
        rows = tq - r0
        off = pl.multiple_of(j * tk, tk)
        if masked:
            diff = (lax.broadcasted_iota(jnp.int32, (rows, tk), 0)
                    - lax.broadcasted_iota(jnp.int32, (rows, tk), 1))
            mask = diff > (j * tk - qi * tq - r0)
        lanes = [slice(s * HEAD_DIM, (s + 1) * HEAD_DIM) for s in range(hs)]
        zs = []
        for s in range(hs):
            ks = k_ref[0, pl.ds(off, tk), lanes[s]]
            zs.append(lax.dot_general(q_ref[0, r0:tq, lanes[s]], ks, NT_DIMS,
                                      preferred_element_type=F32) + sb_ref[hg * hs + s])
        css = []
        for s in range(hs):
            sp = _softplus2(zs[s])
            if masked:
                sp = jnp.where(mask, sp, 0.0)
            css.append(jnp.dot(sp.astype(BF16), nst, preferred_element_type=F32))
        for s in range(hs):
            run = run_scr[s, r0:tq, :]
            att = jnp.exp2(zs[s] + css[s] + jnp.concatenate([run] * (tk // LANES), axis=1))
            if masked:
                att = jnp.where(mask, att, 0.0)
            vs = v_ref[0, pl.ds(off, tk), lanes[s]]
            acc_scr[s, r0:tq, :] += jnp.dot(att.astype(BF16), vs, preferred_element_type=F32)
            run_scr[s, r0:tq, :] = run + jnp.broadcast_to(css[s][:, 0:1], (rows, LANES))

    for d in reversed(range(sub)):
        block(qi * sub + d, d * tk, True)

    def body(jj, carry):
        block(qi * sub - 1 - jj, 0, False)
        return carry

    lax.fori_loop(0, qi * sub, body, 0)
    for s in range(hs):
        o_ref[0, :, s * HEAD_DIM:(s + 1) * HEAD_DIM] = acc_scr[s].astype(BF16)


def _sb_attn_seq(q3, kb3, vb3, sb2, tq=256, tk=256, hs=8):
    bsz, tp, _ = q3.shape
    wl = hs * HEAD_DIM
    grid_spec = pltpu.PrefetchScalarGridSpec(
        num_scalar_prefetch=1,
        grid=(bsz, N_HEADS // hs, tp // tq),
        in_specs=[
            pl.BlockSpec((1, tq, wl), lambda b, h, i, sbr: (b, i, h)),
            pl.BlockSpec((1, tp, wl), lambda b, h, i, sbr: (b, 0, h)),
            pl.BlockSpec((1, tp, wl), lambda b, h, i, sbr: (b, 0, h)),
            pl.BlockSpec((tk, tk), lambda b, h, i, sbr: (0, 0)),
        ],
        out_specs=pl.BlockSpec((1, tq, wl), lambda b, h, i, sbr: (b, i, h)),
        scratch_shapes=[pltpu.VMEM((hs, tq, HEAD_DIM), F32), pltpu.VMEM((hs, tq, LANES), F32)],
    )
    return pl.pallas_call(
        functools.partial(_sb_attn_seq_kernel, tq=tq, tk=tk, hs=hs),
        grid_spec=grid_spec,
        out_shape=jax.ShapeDtypeStruct(q3.shape, BF16),
        compiler_params=_cparams(("parallel", "parallel", "arbitrary")),
        name="sb_attn_seq",
    )(sb2, q3, kb3, vb3, _neg_suffix(tk))


PAGE_ROWS = PAGE_SIZE * N_HEADS
PAGES_PER_STEP = 4


def _sb_attn_paged_kernel(pt_ref, q_ref, bias_ref, *refs, n_steps, pp):
    k_refs, v_refs = refs[:pp], refs[pp:2 * pp]
    nst_ref, o_ref, acc_scr, run_scr = refs[2 * pp:]
    step = pl.program_id(1)

    @pl.when(step == 0)
    def _():
        acc_scr[...] = jnp.zeros_like(acc_scr)
        run_scr[...] = jnp.zeros_like(run_scr)

    n_tiles = PAGE_ROWS // LANES
    q = q_ref[0]
    bias = jnp.concatenate([bias_ref[...]] * n_tiles, axis=1)
    own = ((lax.broadcasted_iota(jnp.int32, (N_HEADS, PAGE_ROWS), 1) & (N_HEADS - 1))
           == lax.broadcasted_iota(jnp.int32, (N_HEADS, PAGE_ROWS), 0))
    nst = nst_ref[...]
    run = run_scr[...]
    acc = acc_scr[...]
    for i in range(pp):
        kb = k_refs[i][0].astype(BF16)
        z = lax.dot_general(q, kb, NT_DIMS, preferred_element_type=F32) + bias
        sp = jnp.where(own, _softplus2(z), 0.0)
        sp_st = jnp.concatenate([sp[:, t * LANES:(t + 1) * LANES] for t in range(n_tiles)], axis=0)
        cs = jnp.dot(sp_st.astype(BF16), nst, preferred_element_type=F32)
        logw = [None] * n_tiles
        for t in reversed(range(n_tiles)):
            rows = slice(t * N_HEADS, (t + 1) * N_HEADS)
            logw[t] = z[:, t * LANES:(t + 1) * LANES] + cs[rows, :LANES] + run
            run = run + cs[rows, LANES:]
        att = jnp.where(own, jnp.exp2(jnp.concatenate(logw, axis=1)), 0.0)
        vb = v_refs[i][0].astype(BF16)
        acc = acc + jnp.dot(att.astype(BF16), vb, preferred_element_type=F32)
    acc_scr[...] = acc
    run_scr[...] = run

    @pl.when(step == n_steps - 1)
    def _():
        o_ref[0] = acc


def _sb_attn_paged(q, cache_k, cache_v, page_table, sb2):
    assert N_HEADS & (N_HEADS - 1) == 0 and LANES % N_HEADS == 0
    nb, n_pages = page_table.shape
    n_pool = cache_k.shape[0]
    pp = PAGES_PER_STEP
    n_steps = n_pages // pp
    assert n_steps * pp == n_pages
    ck = cache_k.reshape(n_pool, PAGE_ROWS, HEAD_DIM)
    cv = cache_v.reshape(n_pool, PAGE_ROWS, HEAD_DIM)
    q3 = q.reshape(nb, N_HEADS, HEAD_DIM)
    bias = jnp.broadcast_to(sb2.astype(F32)[:, None], (N_HEADS, LANES))

    def page_spec(i):
        return pl.BlockSpec((1, PAGE_ROWS, HEAD_DIM),
                            lambda b, s, pt: (pt[b, n_pages - 1 - (s * pp + i)], 0, 0))

    grid_spec = pltpu.PrefetchScalarGridSpec(
        num_scalar_prefetch=1,
        grid=(nb, n_steps),
        in_specs=[
            pl.BlockSpec((1, N_HEADS, HEAD_DIM), lambda b, s, pt: (b, 0, 0)),
            pl.BlockSpec((N_HEADS, LANES), lambda b, s, pt: (0, 0)),
            *[page_spec(i) for i in range(pp)],
            *[page_spec(i) for i in range(pp)],
            pl.BlockSpec((LANES, 2 * LANES), lambda b, s, pt: (0, 0)),
        ],
        out_specs=pl.BlockSpec((1, N_HEADS, HEAD_DIM), lambda b, s, pt: (b, 0, 0)),
        scratch_shapes=[pltpu.VMEM((N_HEADS, HEAD_DIM), F32), pltpu.VMEM((N_HEADS, LANES), F32)],
    )
    o = pl.pallas_call(
        functools.partial(_sb_attn_paged_kernel, n_steps=n_steps, pp=pp),
        grid_spec=grid_spec,
        out_shape=jax.ShapeDtypeStruct((nb, N_HEADS, HEAD_DIM), F32),
        compiler_params=_cparams(("parallel", "arbitrary")),
        name="sb_attn_paged",
    )(page_table, q3, bias, *([ck] * pp), *([cv] * pp), _neg_suffix_and_total(LANES))
    return o.reshape(nb, D_MODEL)


T_PAD = 4352
TM_P = 1088
TM_DOWN = 544
TN_NORM = 256
TN_MM = 512
SAMPLE_ROWS = 16


def _prompt_trunk(x, wts):
    bsz, t_real, _ = x.shape
    pad = T_PAD - t_real
    x = jnp.pad(x, ((0, 0), (0, pad), (0, 0))).reshape(bsz * T_PAD, D_MODEL)
    tiles_per_seq = T_PAD // TM_P
    tail_row = (t_real - 1) % TM_P - (SUBLANES - 1)
    assert (t_real - 1) // TM_P == tiles_per_seq - 1 and tail_row % SUBLANES == 0
    conv_states, ffn_states = [], []
    k = v = kb = vb = None
    for l in range(DEPTH):
        if l < N_A_LAYERS:
            u = _glu(x, wts["a_g_norm"][l], wts["a_w_pw1"][l], TM_P, TN_NORM)
            u3 = u.reshape(bsz, T_PAD, D_MODEL)
            conv_states.append(u3[:, t_real - (CONV_A_WIDTH - 1):t_real])
            c = _dwconv_seq(u3, wts["a_w_dw"][l], wts["a_b_dw"][l])
            x = _norm_silu_mm_res(c.reshape(bsz * T_PAD, D_MODEL), wts["a_g_cn"][l],
                                  wts["a_w_pw2"][l], x, TM_P, TN_NORM)
        else:
            j = l - N_A_LAYERS
            q = _q_proj(x, wts["b_g_norm"][j], wts["b_w_q"][j], wts["b_g_q"][j], TM_P, TN_NORM)
            o = _sb_attn_seq(q.reshape(bsz, T_PAD, D_MODEL), kb, vb, wts["b_sb"][j] * LOG2E)
            x = _mm_res(o.reshape(bsz * T_PAD, D_MODEL), wts["b_w_o"][j], x, TM_P, TN_MM, "o_proj")
        act, tail = _ffn_gate_seq(x, wts["f_g_norm"][l], wts["f_w_gate"][l], wts["f_w_up"][l],
                                  wts["f_w_conv"][l], wts["f_b_conv"][l], TM_P, TN_NORM,
                                  tiles_per_seq, tail_row)
        tail = tail.reshape(bsz, tiles_per_seq, SUBLANES, D_FF)
        ffn_states.append(tail[:, tiles_per_seq - 1, SUBLANES - (FFN_CONV_WIDTH - 1):])
        x = _mm_res(act, wts["f_w_down"][l], x, TM_DOWN, TN_MM, "ffn_down")
        if l == N_A_LAYERS - 1:
            k, v, kb, vb = _kv_proj(x, wts["kv_g_norm"], wts["w_kv"], wts["g_k"], TM_P, TN_NORM)
            kb = kb.reshape(bsz, T_PAD, D_MODEL)
            vb = vb.reshape(bsz, T_PAD, D_MODEL)
    x = x.reshape(bsz, T_PAD, D_MODEL)[:, N_META:t_real]
    k = k.reshape(bsz, T_PAD, N_HEADS, HEAD_DIM)[:, :t_real]
    v = v.reshape(bsz, T_PAD, N_HEADS, HEAD_DIM)[:, :t_real]
    return x, jnp.stack(conv_states), jnp.stack(ffn_states), k, v


def _sample_trunk(x, state_conv_a, state_ffn, cache_k, cache_v, page_table, wts):
    nb = x.shape[0]
    m = SAMPLE_ROWS

    def pad_rows(a):
        return jnp.pad(a, ((0, m - nb), (0, 0)))

    x = pad_rows(x.reshape(nb, D_MODEL))
    conv_states, ffn_states = [], []
    k = v = None
    for l in range(DEPTH):
        if l < N_A_LAYERS:
            u = _glu(x, wts["a_g_norm"][l], wts["a_w_pw1"][l], m, TN_MM)[:nb]
            past = state_conv_a[l]
            conv_states.append(jnp.concatenate([past[:, 1:], u[:, None]], axis=1))
            c = _dwconv_step(past, u, wts["a_w_dw"][l], wts["a_b_dw"][l])
            x = _norm_silu_mm_res(pad_rows(c), wts["a_g_cn"][l], wts["a_w_pw2"][l], x, m, TN_MM)
        else:
            j = l - N_A_LAYERS
            q = _q_proj(x, wts["b_g_norm"][j], wts["b_w_q"][j], wts["b_g_q"][j], m, TN_MM)
            o = _sb_attn_paged(q[:nb], cache_k, cache_v, page_table, wts["b_sb"][j] * LOG2E)
            x = _mm_res(pad_rows(o).astype(BF16), wts["b_w_o"][j], x, m, TN_MM, "o_proj")
        past = state_ffn[l]
        act, gate = _ffn_gate_step(x, wts["f_g_norm"][l], wts["f_w_gate"][l], wts["f_w_up"][l],
                                   wts["f_w_conv"][l], wts["f_b_conv"][l],
                                   pad_rows(past[:, 0]), pad_rows(past[:, 1]), TN_MM)
        ffn_states.append(jnp.stack([past[:, 1], gate[:nb]], axis=1))
        x = _mm_res(act, wts["f_w_down"][l], x, m, TN_MM, "ffn_down")
        if l == N_A_LAYERS - 1:
            k, v, _, _ = _kv_proj(x, wts["kv_g_norm"], wts["w_kv"], wts["g_k"], m, TN_MM)
    y = x[:nb].reshape(nb, 1, D_MODEL)
    k = k[:nb].reshape(nb, 1, N_HEADS, HEAD_DIM)
    v = v[:nb].reshape(nb, 1, N_HEADS, HEAD_DIM)
    return y, jnp.stack(conv_states), jnp.stack(ffn_states), k, v


_MATMUL_WEIGHTS = ("a_w_pw1", "a_w_pw2", "f_w_gate", "f_w_up", "f_w_down", "w_kv", "b_w_q", "b_w_o")


def kernel(x_prompt, x_sample, state_conv_a, state_ffn_conv, cache_k, cache_v, page_table, meta_tokens, a_g_norm, a_w_pw1, a_w_dw, a_b_dw, a_g_cn, a_w_pw2, f_g_norm, f_w_gate, f_w_up, f_w_down, f_w_conv, f_b_conv, kv_g_norm, w_kv, g_k, b_g_norm, b_w_q, b_g_q, b_sb, b_w_o):
    wts = dict(a_g_norm=a_g_norm, a_w_pw1=a_w_pw1, a_w_dw=a_w_dw, a_b_dw=a_b_dw, a_g_cn=a_g_cn,
               a_w_pw2=a_w_pw2, f_g_norm=f_g_norm, f_w_gate=f_w_gate, f_w_up=f_w_up,
               f_w_down=f_w_down, f_w_conv=f_w_conv, f_b_conv=f_b_conv, kv_g_norm=kv_g_norm,
               w_kv=w_kv, g_k=g_k, b_g_norm=b_g_norm, b_w_q=b_w_q, b_g_q=b_g_q, b_sb=b_sb,
               b_w_o=b_w_o)
    for name in _MATMUL_WEIGHTS:
        wts[name] = wts[name].astype(BF16)
    bp = x_prompt.shape[0]
    meta = jnp.broadcast_to(meta_tokens[None], (bp, N_META, D_MODEL))
    xp = jnp.concatenate([meta, x_prompt], axis=1)
    y_p, conv_p, ffn_p, k_p, v_p = _prompt_trunk(xp, wts)
    y_s, conv_s, ffn_s, k_s, v_s = _sample_trunk(x_sample, state_conv_a, state_ffn_conv,
                                                 cache_k, cache_v, page_table, wts)
    return (y_p, y_s, conv_p, ffn_p, k_p, v_p, conv_s, ffn_s, k_s, v_s)
```

```python
import functools

import jax
import jax.numpy as jnp
from jax import lax
from jax.experimental import pallas as pl
from jax.experimental.pallas import tpu as pltpu

D_MODEL = 2048
D_FF = 5632
N_HEADS = 16
HEAD_DIM = 128
N_META = 16
N_A_LAYERS = 2
DEPTH = 4
CONV_A_WIDTH = 31
FFN_CONV_WIDTH = 3
PAGE_SIZE = 128
EPS = 1e-6

LANES = 128
SUBLANES = 8
VMEM_LIMIT = 56 * 1024 * 1024

F32 = jnp.float32
BF16 = jnp.bfloat16


def _cparams(sem):
    return pltpu.CompilerParams(dimension_semantics=sem, vmem_limit_bytes=VMEM_LIMIT)


def _rms_rows(x, g):
    ms = jnp.mean(x * x, axis=-1, keepdims=True)
    return x * lax.rsqrt(ms + EPS) * g


def _head_rms(t, gh):
    parts = []
    for s in range(t.shape[1] // HEAD_DIM):
        parts.append(_rms_rows(t[:, s * HEAD_DIM:(s + 1) * HEAD_DIM], gh))
    return parts


LOG2E = 1.4426950408889634

Q_SCALE = HEAD_DIM ** -0.5 * LOG2E


def _softplus2(z2):
    return jnp.maximum(z2, 0.0) + jnp.log(1.0 + jnp.exp2(-jnp.abs(z2))) * LOG2E


ROW_CHUNK = 272


MXU_COLS = 256


def _sub_blocks(tm, tn):
    rc = ROW_CHUNK if tm % ROW_CHUNK == 0 else tm
    cc = MXU_COLS if tn % MXU_COLS == 0 else tn
    return [(slice(r0, r0 + rc), slice(c0, c0 + cc))
            for c0 in range(0, tn, cc) for r0 in range(0, tm, rc)]


def _glu_kernel(x_ref, g_ref, wa_ref, wg_ref, u_ref, h_scr):
    @pl.when(pl.program_id(1) == 0)
    def _():
        h_scr[...] = _rms_rows(x_ref[...], g_ref[...]).astype(BF16)

    for rows, cols in _sub_blocks(*u_ref.shape):
        h = h_scr[rows, :]
        a = jnp.dot(h, wa_ref[:, cols], preferred_element_type=F32)
        gt = jnp.dot(h, wg_ref[:, cols], preferred_element_type=F32)
        u_ref[rows, cols] = a * jax.nn.sigmoid(gt)


def _glu(x, g, w_bf, tm, tn):
    m = x.shape[0]
    nj = D_MODEL // tn
    return pl.pallas_call(
        _glu_kernel,
        grid=(m // tm, nj),
        in_specs=[
            pl.BlockSpec((tm, D_MODEL), lambda i, j: (i, 0)),
            pl.BlockSpec((1, D_MODEL), lambda i, j: (0, 0)),
            pl.BlockSpec((D_MODEL, tn), lambda i, j: (0, j)),
            pl.BlockSpec((D_MODEL, tn), lambda i, j: (0, j + nj)),
        ],
        out_specs=pl.BlockSpec((tm, tn), lambda i, j: (i, j)),
        out_shape=jax.ShapeDtypeStruct((m, D_MODEL), F32),
        scratch_shapes=[pltpu.VMEM((tm, D_MODEL), BF16)],
        compiler_params=_cparams(("parallel", "arbitrary")),
        name="glu",
    )(x, g.reshape(1, D_MODEL), w_bf, w_bf)


DW_HALO = 32


def _dwconv_seq_kernel(u_ref, halo_ref, w_ref, b_ref, o_ref, full_scr, sh_scr, *, tt, cl):
    first = pl.program_id(1) == 0
    full_scr[0:DW_HALO, :] = jnp.where(first, 0.0, halo_ref[0])
    full_scr[DW_HALO:DW_HALO + tt, :] = u_ref[0]
    lead = DW_HALO - (CONV_A_WIDTH - 1)
    sh_rows = sh_scr.shape[1]
    for c in range(cl // LANES):
        ls = slice(c * LANES, (c + 1) * LANES)
        for s in range(1, SUBLANES):
            sh_scr[s - 1, :, ls] = full_scr[s:s + sh_rows, ls]
        acc = jnp.broadcast_to(b_ref[:, ls], (tt, LANES))
        for k in range(CONV_A_WIDTH):
            s, a0 = (lead + k) % SUBLANES, (lead + k) // SUBLANES * SUBLANES
            rows = full_scr[a0:a0 + tt, ls] if s == 0 else sh_scr[s - 1, a0:a0 + tt, ls]
            acc = acc + w_ref[k:k + 1, ls] * rows
        o_ref[0, :, ls] = acc


def _dwconv_seq(u3, w, b, tt=128, cl=1024):
    bsz, tp, _ = u3.shape
    w_pad = jnp.zeros((DW_HALO, D_MODEL), F32).at[:CONV_A_WIDTH].set(w)
    hb = tt // DW_HALO
    return pl.pallas_call(
        functools.partial(_dwconv_seq_kernel, tt=tt, cl=cl),
        grid=(bsz, tp // tt, D_MODEL // cl),
        in_specs=[
            pl.BlockSpec((1, tt, cl), lambda bb, i, c: (bb, i, c)),
            pl.BlockSpec((1, DW_HALO, cl), lambda bb, i, c: (bb, jnp.maximum(i * hb - 1, 0), c)),
            pl.BlockSpec((DW_HALO, cl), lambda bb, i, c: (0, c)),
            pl.BlockSpec((1, cl), lambda bb, i, c: (0, c)),
        ],
        out_specs=pl.BlockSpec((1, tt, cl), lambda bb, i, c: (bb, i, c)),
        out_shape=jax.ShapeDtypeStruct(u3.shape, F32),
        scratch_shapes=[pltpu.VMEM((DW_HALO + tt, cl), F32),
                        pltpu.VMEM((SUBLANES - 1, DW_HALO + tt - SUBLANES, cl), F32)],
        compiler_params=_cparams(("parallel", "parallel", "parallel")),
        name="dwconv_seq",
    )(u3, u3, w_pad, b.reshape(1, D_MODEL))


def _dwconv_step_kernel(s_ref, u_ref, w_ref, b_ref, o_ref):
    w = w_ref[...]
    past = jnp.sum(s_ref[...] * w[None, :CONV_A_WIDTH - 1, :], axis=1)
    o_ref[...] = past + u_ref[...] * w[CONV_A_WIDTH - 1:CONV_A_WIDTH, :] + b_ref[...]


def _dwconv_step(state, u, w, b):
    nb = state.shape[0]
    return pl.pallas_call(
        _dwconv_step_kernel,
        out_shape=jax.ShapeDtypeStruct((nb, D_MODEL), F32),
        compiler_params=pltpu.CompilerParams(vmem_limit_bytes=VMEM_LIMIT),
        name="dwconv_step",
    )(state, u, w, b.reshape(1, D_MODEL))


def _norm_silu_mm_res_kernel(c_ref, g_ref, w_ref, x_ref, o_ref, h_scr):
    @pl.when(pl.program_id(1) == 0)
    def _():
        y = _rms_rows(c_ref[...], g_ref[...])
        h_scr[...] = (y * jax.nn.sigmoid(y)).astype(BF16)

    for rows, cols in _sub_blocks(*o_ref.shape):
        o_ref[rows, cols] = x_ref[rows, cols] + jnp.dot(h_scr[rows, :], w_ref[:, cols],
                                                        preferred_element_type=F32)


def _norm_silu_mm_res(c, g, w_bf, xres, tm, tn):
    m = c.shape[0]
    return pl.pallas_call(
        _norm_silu_mm_res_kernel,
        grid=(m // tm, D_MODEL // tn),
        in_specs=[
            pl.BlockSpec((tm, D_MODEL), lambda i, j: (i, 0)),
            pl.BlockSpec((1, D_MODEL), lambda i, j: (0, 0)),
            pl.BlockSpec((D_MODEL, tn), lambda i, j: (0, j)),
            pl.BlockSpec((tm, tn), lambda i, j: (i, j)),
        ],
        out_specs=pl.BlockSpec((tm, tn), lambda i, j: (i, j)),
        out_shape=jax.ShapeDtypeStruct((m, D_MODEL), F32),
        scratch_shapes=[pltpu.VMEM((tm, D_MODEL), BF16)],
        compiler_params=_cparams(("parallel", "arbitrary")),
        name="pw2",
    )(c, g.reshape(1, D_MODEL), w_bf, xres)


def _ffn_gate_seq_kernel(x_ref, xp_ref, g_ref, wg_ref, wu_ref, wc_ref, bc_ref,
                         act_ref, tail_ref, h_scr, hp_scr, gs_scr, *, tm, tiles_per_seq, tail_row):
    i = pl.program_id(0)

    @pl.when(pl.program_id(1) == 0)
    def _():
        h_scr[...] = _rms_rows(x_ref[...], g_ref[...]).astype(BF16)
        hp = _rms_rows(xp_ref[...], g_ref[...])
        hp = jnp.where(i % tiles_per_seq == 0, 0.0, hp)
        hp_scr[...] = jnp.concatenate([hp, hp], axis=0).astype(BF16)

    gprev = jnp.dot(hp_scr[...], wg_ref[...], preferred_element_type=F32)
    gs_scr[0:SUBLANES, :] = gprev[0:SUBLANES]
    for rows, cols in _sub_blocks(*act_ref.shape):
        r0, r1 = rows.start, rows.stop
        h = h_scr[rows, :]
        gate = jnp.dot(h, wg_ref[:, cols], preferred_element_type=F32)
        up = jnp.dot(h, wu_ref[:, cols], preferred_element_type=F32)
        gs_scr[SUBLANES + r0:SUBLANES + r1, cols] = gate
        g1 = gs_scr[SUBLANES - 1 + r0:SUBLANES - 1 + r1, cols]
        g2 = gs_scr[SUBLANES - 2 + r0:SUBLANES - 2 + r1, cols]
        gc = (wc_ref[0:1, cols] * g2 + wc_ref[1:2, cols] * g1 + wc_ref[2:3, cols] * gate
              + bc_ref[:, cols])
        act_ref[rows, cols] = (gc * jax.nn.sigmoid(gc) * up).astype(BF16)
        if r0 <= tail_row < r1:
            tail_ref[0, :, cols] = gate[tail_row - r0:tail_row - r0 + SUBLANES]


def _ffn_gate_seq(x, g, wg_bf, wu_bf, wc, bc, tm, tn, tiles_per_seq, tail_row):
    m = x.shape[0]
    nt = m // tm
    hb = tm // SUBLANES
    wc_pad = jnp.zeros((SUBLANES, D_FF), F32).at[:FFN_CONV_WIDTH].set(wc)
    return pl.pallas_call(
        functools.partial(_ffn_gate_seq_kernel, tm=tm, tiles_per_seq=tiles_per_seq, tail_row=tail_row),
        grid=(nt, D_FF // tn),
        in_specs=[
            pl.BlockSpec((tm, D_MODEL), lambda i, j: (i, 0)),
            pl.BlockSpec((SUBLANES, D_MODEL), lambda i, j: (jnp.maximum(i * hb - 1, 0), 0)),
            pl.BlockSpec((1, D_MODEL), lambda i, j: (0, 0)),
            pl.BlockSpec((D_MODEL, tn), lambda i, j: (0, j)),
            pl.BlockSpec((D_MODEL, tn), lambda i, j: (0, j)),
            pl.BlockSpec((SUBLANES, tn), lambda i, j: (0, j)),
            pl.BlockSpec((1, tn), lambda i, j: (0, j)),
        ],
        out_specs=[
            pl.BlockSpec((tm, tn), lambda i, j: (i, j)),
            pl.BlockSpec((1, SUBLANES, tn), lambda i, j: (i, 0, j)),
        ],
        out_shape=[
            jax.ShapeDtypeStruct((m, D_FF), BF16),
            jax.ShapeDtypeStruct((nt, SUBLANES, D_FF), F32),
        ],
        scratch_shapes=[
            pltpu.VMEM((tm, D_MODEL), BF16),
            pltpu.VMEM((2 * SUBLANES, D_MODEL), BF16),
            pltpu.VMEM((SUBLANES + tm, tn), F32),
        ],
        compiler_params=_cparams(("parallel", "arbitrary")),
        name="ffn_gate_seq",
    )(x, x, g.reshape(1, D_MODEL), wg_bf, wu_bf, wc_pad, bc.reshape(1, D_FF))


def _ffn_gate_step_kernel(x_ref, g_ref, wg_ref, wu_ref, wc_ref, bc_ref, s0_ref, s1_ref,
                          act_ref, gate_ref, h_scr):
    @pl.when(pl.program_id(0) == 0)
    def _():
        h_scr[...] = _rms_rows(x_ref[...], g_ref[...]).astype(BF16)

    h = h_scr[...]
    gate = jnp.dot(h, wg_ref[...], preferred_element_type=F32)
    up = jnp.dot(h, wu_ref[...], preferred_element_type=F32)
    gc = (wc_ref[0:1, :] * s0_ref[...] + wc_ref[1:2, :] * s1_ref[...]
          + wc_ref[2:3, :] * gate + bc_ref[...])
    act_ref[...] = (gc * jax.nn.sigmoid(gc) * up).astype(BF16)
    gate_ref[...] = gate


def _ffn_gate_step(x, g, wg_bf, wu_bf, wc, bc, s0, s1, tn):
    m = x.shape[0]
    wc_pad = jnp.zeros((SUBLANES, D_FF), F32).at[:FFN_CONV_WIDTH].set(wc)
    return pl.pallas_call(
        _ffn_gate_step_kernel,
        grid=(D_FF // tn,),
        in_specs=[
            pl.BlockSpec((m, D_MODEL), lambda j: (0, 0)),
            pl.BlockSpec((1, D_MODEL), lambda j: (0, 0)),
            pl.BlockSpec((D_MODEL, tn), lambda j: (0, j)),
            pl.BlockSpec((D_MODEL, tn), lambda j: (0, j)),
            pl.BlockSpec((SUBLANES, tn), lambda j: (0, j)),
            pl.BlockSpec((1, tn), lambda j: (0, j)),
            pl.BlockSpec((m, tn), lambda j: (0, j)),
            pl.BlockSpec((m, tn), lambda j: (0, j)),
        ],
        out_specs=[
            pl.BlockSpec((m, tn), lambda j: (0, j)),
            pl.BlockSpec((m, tn), lambda j: (0, j)),
        ],
        out_shape=[
            jax.ShapeDtypeStruct((m, D_FF), BF16),
            jax.ShapeDtypeStruct((m, D_FF), F32),
        ],
        scratch_shapes=[pltpu.VMEM((m, D_MODEL), BF16)],
        compiler_params=_cparams(("arbitrary",)),
        name="ffn_gate_step",
    )(x, g.reshape(1, D_MODEL), wg_bf, wu_bf, wc_pad, bc.reshape(1, D_FF), s0, s1)


def _mm_res_kernel(a_ref, w_ref, x_ref, o_ref):
    for rows, cols in _sub_blocks(*o_ref.shape):
        o_ref[rows, cols] = x_ref[rows, cols] + jnp.dot(a_ref[rows, :], w_ref[:, cols],
                                                        preferred_element_type=F32)


def _mm_res(a_bf, w_bf, xres, tm, tn, name):
    m, kdim = a_bf.shape
    n = w_bf.shape[1]
    return pl.pallas_call(
        _mm_res_kernel,
        grid=(m // tm, n // tn),
        in_specs=[
            pl.BlockSpec((tm, kdim), lambda i, j: (i, 0)),
            pl.BlockSpec((kdim, tn), lambda i, j: (0, j)),
            pl.BlockSpec((tm, tn), lambda i, j: (i, j)),
        ],
        out_specs=pl.BlockSpec((tm, tn), lambda i, j: (i, j)),
        out_shape=jax.ShapeDtypeStruct((m, n), F32),
        compiler_params=_cparams(("parallel", "parallel")),
        name=name,
    )(a_bf, w_bf, xres)


def _kv_kernel(x_ref, g_ref, wk_ref, wv_ref, gk_ref, k_ref, v_ref, kb_ref, vb_ref, h_scr):
    @pl.when(pl.program_id(1) == 0)
    def _():
        h_scr[...] = _rms_rows(x_ref[...], g_ref[...]).astype(BF16)

    for rows, cols in _sub_blocks(*k_ref.shape):
        h = h_scr[rows, :]
        kraw = jnp.dot(h, wk_ref[:, cols], preferred_element_type=F32)
        v = jnp.dot(h, wv_ref[:, cols], preferred_element_type=F32)
        for s, kn in enumerate(_head_rms(kraw, gk_ref[...])):
            hc = slice(cols.start + s * HEAD_DIM, cols.start + (s + 1) * HEAD_DIM)
            k_ref[rows, hc] = kn
            kb_ref[rows, hc] = kn.astype(BF16)
        v_ref[rows, cols] = v
        vb_ref[rows, cols] = v.astype(BF16)


def _kv_proj(x, g, w_bf, gk, tm, tn):
    m = x.shape[0]
    nj = D_MODEL // tn
    blk = pl.BlockSpec((tm, tn), lambda i, j: (i, j))
    return pl.pallas_call(
        _kv_kernel,
        grid=(m // tm, nj),
        in_specs=[
            pl.BlockSpec((tm, D_MODEL), lambda i, j: (i, 0)),
            pl.BlockSpec((1, D_MODEL), lambda i, j: (0, 0)),
            pl.BlockSpec((D_MODEL, tn), lambda i, j: (0, j)),
            pl.BlockSpec((D_MODEL, tn), lambda i, j: (0, j + nj)),
            pl.BlockSpec((1, HEAD_DIM), lambda i, j: (0, 0)),
        ],
        out_specs=[blk, blk, blk, blk],
        out_shape=[
            jax.ShapeDtypeStruct((m, D_MODEL), F32),
            jax.ShapeDtypeStruct((m, D_MODEL), F32),
            jax.ShapeDtypeStruct((m, D_MODEL), BF16),
            jax.ShapeDtypeStruct((m, D_MODEL), BF16),
        ],
        scratch_shapes=[pltpu.VMEM((tm, D_MODEL), BF16)],
        compiler_params=_cparams(("parallel", "arbitrary")),
        name="kv_proj",
    )(x, g.reshape(1, D_MODEL), w_bf, w_bf, gk.reshape(1, HEAD_DIM))


def _q_kernel(x_ref, g_ref, w_ref, gq_ref, q_ref, h_scr):
    @pl.when(pl.program_id(1) == 0)
    def _():
        h_scr[...] = _rms_rows(x_ref[...], g_ref[...]).astype(BF16)

    for rows, cols in _sub_blocks(*q_ref.shape):
        qraw = jnp.dot(h_scr[rows, :], w_ref[:, cols], preferred_element_type=F32)
        for s, qn in enumerate(_head_rms(qraw, gq_ref[...])):
            hc = slice(cols.start + s * HEAD_DIM, cols.start + (s + 1) * HEAD_DIM)
            q_ref[rows, hc] = (qn * Q_SCALE).astype(BF16)


def _q_proj(x, g, w_bf, gq, tm, tn):
    m = x.shape[0]
    return pl.pallas_call(
        _q_kernel,
        grid=(m // tm, D_MODEL // tn),
        in_specs=[
            pl.BlockSpec((tm, D_MODEL), lambda i, j: (i, 0)),
            pl.BlockSpec((1, D_MODEL), lambda i, j: (0, 0)),
            pl.BlockSpec((D_MODEL, tn), lambda i, j: (0, j)),
            pl.BlockSpec((1, HEAD_DIM), lambda i, j: (0, 0)),
        ],
        out_specs=pl.BlockSpec((tm, tn), lambda i, j: (i, j)),
        out_shape=jax.ShapeDtypeStruct((m, D_MODEL), BF16),
        scratch_shapes=[pltpu.VMEM((tm, D_MODEL), BF16)],
        compiler_params=_cparams(("parallel", "arbitrary")),
        name="q_proj",
    )(x, g.reshape(1, D_MODEL), w_bf, gq.reshape(1, HEAD_DIM))


NT_DIMS = (((1,), (1,)), ((), ()))


def _neg_suffix(tk):
    r = lax.broadcasted_iota(jnp.int32, (tk, tk), 0)
    c = lax.broadcasted_iota(jnp.int32, (tk, tk), 1)
    return -(r >= c).astype(BF16)


def _neg_suffix_and_total(tk):
    return jnp.concatenate([_neg_suffix(tk), -jnp.ones((tk, tk), BF16)], axis=1)


def _sb_attn_seq_kernel(sb_ref, q_ref, k_ref, v_ref, nst_ref, o_ref, acc_scr, run_scr,
                        *, tq, tk, hs):
    hg = pl.program_id(1)
    qi = pl.program_id(2)
    sub = tq // tk
    nst = nst_ref[...]
    acc_scr[...] = jnp.zeros_like(acc_scr)
    run_scr[...] = jnp.zeros_like(run_scr)

    def block(j, r0, masked):
        rows = tq - r0
        off = pl.multiple_of(j * tk, tk)
        if masked:
            diff = (lax.broadcasted_iota(jnp.int32, (rows, tk), 0)
                    - lax.broadcasted_iota(jnp.int32, (rows, tk), 1))
            mask = diff > (j * tk - qi * tq - r0)
        lanes = [slice(s * HEAD_DIM, (s + 1) * HEAD_DIM) for s in range(hs)]
        zs = []
        for s in range(hs):
            ks = k_ref[0, pl.ds(off, tk), lanes[s]]
            zs.append(lax.dot_general(q_ref[0, r0:tq, lanes[s]], ks, NT_DIMS,
                                      preferred_element_type=F32) + sb_ref[hg * hs + s])
        css = []
        for s in range(hs):
            sp = _softplus2(zs[s])
            if masked:
                sp = jnp.where(mask, sp, 0.0)
            css.append(jnp.dot(sp.astype(BF16), nst, preferred_element_type=F32))
        for s in range(hs):
            run = run_scr[s, r0:tq, :]
            att = jnp.exp2(zs[s] + css[s] + jnp.concatenate([run] * (tk // LANES), axis=1))
            if masked:
                att = jnp.where(mask, att, 0.0)
            vs = v_ref[0, pl.ds(off, tk), lanes[s]]
            acc_scr[s, r0:tq, :] += jnp.dot(att.astype(BF16), vs, preferred_element_type=F32)
            run_scr[s, r0:tq, :] = run + jnp.broadcast_to(css[s][:, 0:1], (rows, LANES))

    for d in reversed(range(sub)):
        block(qi * sub + d, d * tk, True)

    def body(jj, carry):
        block(qi * sub - 1 - jj, 0, False)
        return carry

    lax.fori_loop(0, qi * sub, body, 0)
    for s in range(hs):
        o_ref[0, :, s * HEAD_DIM:(s + 1) * HEAD_DIM] = acc_scr[s].astype(BF16)


def _sb_attn_seq(q3, kb3, vb3, sb2, tq=256, tk=256, hs=8):
    bsz, tp, _ = q3.shape
    wl = hs * HEAD_DIM
    grid_spec = pltpu.PrefetchScalarGridSpec(
        num_scalar_prefetch=1,
        grid=(bsz, N_HEADS // hs, tp // tq),
        in_specs=[
            pl.BlockSpec((1, tq, wl), lambda b, h, i, sbr: (b, i, h)),
            pl.BlockSpec((1, tp, wl), lambda b, h, i, sbr: (b, 0, h)),
            pl.BlockSpec((1, tp, wl), lambda b, h, i, sbr: (b, 0, h)),
            pl.BlockSpec((tk, tk), lambda b, h, i, sbr: (0, 0)),
        ],
        out_specs=pl.BlockSpec((1, tq, wl), lambda b, h, i, sbr: (b, i, h)),
        scratch_shapes=[pltpu.VMEM((hs, tq, HEAD_DIM), F32), pltpu.VMEM((hs, tq, LANES), F32)],
    )
    return pl.pallas_call(
        functools.partial(_sb_attn_seq_kernel, tq=tq, tk=tk, hs=hs),
        grid_spec=grid_spec,
        out_shape=jax.ShapeDtypeStruct(q3.shape, BF16),
        compiler_params=_cparams(("parallel", "parallel", "arbitrary")),
        name="sb_attn_seq",
    )(sb2, q3, kb3, vb3, _neg_suffix(tk))


PAGE_ROWS = PAGE_SIZE * N_HEADS
PAGES_PER_STEP = 4


def _sb_attn_paged_kernel(pt_ref, q_ref, bias_ref, *refs, n_steps, pp):
    k_refs, v_refs = refs[:pp], refs[pp:2 * pp]
    nst_ref, o_ref, acc_scr, run_scr = refs[2 * pp:]
    step = pl.program_id(1)

    @pl.when(step == 0)
    def _():
        acc_scr[...] = jnp.zeros_like(acc_scr)
        run_scr[...] = jnp.zeros_like(run_scr)

    n_tiles = PAGE_ROWS // LANES
    q = q_ref[0]
    bias = jnp.concatenate([bias_ref[...]] * n_tiles, axis=1)
    own = ((lax.broadcasted_iota(jnp.int32, (N_HEADS, PAGE_ROWS), 1) & (N_HEADS - 1))
           == lax.broadcasted_iota(jnp.int32, (N_HEADS, PAGE_ROWS), 0))
    nst = nst_ref[...]
    run = run_scr[...]
    acc = acc_scr[...]
    zs = []
    for i in range(pp):
        kb = k_refs[i][0].astype(BF16)
        zs.append(lax.dot_general(q, kb, NT_DIMS, preferred_element_type=F32) + bias)
    css = []
    for i in range(pp):
        sp = jnp.where(own, _softplus2(zs[i]), 0.0)
        sp_st = jnp.concatenate([sp[:, t * LANES:(t + 1) * LANES] for t in range(n_tiles)], axis=0)
        css.append(jnp.dot(sp_st.astype(BF16), nst, preferred_element_type=F32))
    for i in range(pp):
        logw = [None] * n_tiles
        for t in reversed(range(n_tiles)):
            rows = slice(t * N_HEADS, (t + 1) * N_HEADS)
            logw[t] = zs[i][:, t * LANES:(t + 1) * LANES] + css[i][rows, :LANES] + run
            run = run + css[i][rows, LANES:]
        att = jnp.where(own, jnp.exp2(jnp.concatenate(logw, axis=1)), 0.0)
        vb = v_refs[i][0].astype(BF16)
        acc = acc + jnp.dot(att.astype(BF16), vb, preferred_element_type=F32)
    acc_scr[...] = acc
    run_scr[...] = run

    @pl.when(step == n_steps - 1)
    def _():
        o_ref[0] = acc


def _sb_attn_paged(q, cache_k, cache_v, page_table, sb2):
    assert N_HEADS & (N_HEADS - 1) == 0 and LANES % N_HEADS == 0
    nb, n_pages = page_table.shape
    n_pool = cache_k.shape[0]
    pp = PAGES_PER_STEP
    n_steps = n_pages // pp
    assert n_steps * pp == n_pages
    ck = cache_k.reshape(n_pool, PAGE_ROWS, HEAD_DIM)
    cv = cache_v.reshape(n_pool, PAGE_ROWS, HEAD_DIM)
    q3 = q.reshape(nb, N_HEADS, HEAD_DIM)
    bias = jnp.broadcast_to(sb2.astype(F32)[:, None], (N_HEADS, LANES))

    def page_spec(i):
        return pl.BlockSpec((1, PAGE_ROWS, HEAD_DIM),
                            lambda b, s, pt: (pt[b, n_pages - 1 - (s * pp + i)], 0, 0))

    grid_spec = pltpu.PrefetchScalarGridSpec(
        num_scalar_prefetch=1,
        grid=(nb, n_steps),
        in_specs=[
            pl.BlockSpec((1, N_HEADS, HEAD_DIM), lambda b, s, pt: (b, 0, 0)),
            pl.BlockSpec((N_HEADS, LANES), lambda b, s, pt: (0, 0)),
            *[page_spec(i) for i in range(pp)],
            *[page_spec(i) for i in range(pp)],
            pl.BlockSpec((LANES, 2 * LANES), lambda b, s, pt: (0, 0)),
        ],
        out_specs=pl.BlockSpec((1, N_HEADS, HEAD_DIM), lambda b, s, pt: (b, 0, 0)),
        scratch_shapes=[pltpu.VMEM((N_HEADS, HEAD_DIM), F32), pltpu.VMEM((N_HEADS, LANES), F32)],
    )
    o = pl.pallas_call(
        functools.partial(_sb_attn_paged_kernel, n_steps=n_steps, pp=pp),
        grid_spec=grid_spec,
        out_shape=jax.ShapeDtypeStruct((nb, N_HEADS, HEAD_DIM), F32),
        compiler_params=_cparams(("parallel", "arbitrary")),
        name="sb_attn_paged",
    )(page_table, q3, bias, *([ck] * pp), *([cv] * pp), _neg_suffix_and_total(LANES))
    return o.reshape(nb, D_MODEL)


T_PAD = 4352
TM_P = 1088
TM_DOWN = 1088
TN_NORM = 512
TN_MM = 512
SAMPLE_ROWS = 16


def _prompt_trunk(x, wts):
    bsz, t_real, _ = x.shape
    pad = T_PAD - t_real
    x = jnp.pad(x, ((0, 0), (0, pad), (0, 0))).reshape(bsz * T_PAD, D_MODEL)
    tiles_per_seq = T_PAD // TM_P
    tail_row = (t_real - 1) % TM_P - (SUBLANES - 1)
    assert (t_real - 1) // TM_P == tiles_per_seq - 1 and tail_row % SUBLANES == 0
    conv_states, ffn_states = [], []
    k = v = kb = vb = None
    for l in range(DEPTH):
        if l < N_A_LAYERS:
            u = _glu(x, wts["a_g_norm"][l], wts["a_w_pw1"][l], TM_P, TN_NORM)
            u3 = u.reshape(bsz, T_PAD, D_MODEL)
            conv_states.append(u3[:, t_real - (CONV_A_WIDTH - 1):t_real])
            c = _dwconv_seq(u3, wts["a_w_dw"][l], wts["a_b_dw"][l])
            x = _norm_silu_mm_res(c.reshape(bsz * T_PAD, D_MODEL), wts["a_g_cn"][l],
                                  wts["a_w_pw2"][l], x, TM_P, TN_NORM)
        else:
            j = l - N_A_LAYERS
            q = _q_proj(x, wts["b_g_norm"][j], wts["b_w_q"][j], wts["b_g_q"][j], TM_P, TN_NORM)
            o = _sb_attn_seq(q.reshape(bsz, T_PAD, D_MODEL), kb, vb, wts["b_sb"][j] * LOG2E)
            x = _mm_res(o.reshape(bsz * T_PAD, D_MODEL), wts["b_w_o"][j], x, TM_P, TN_MM, "o_proj")
        act, tail = _ffn_gate_seq(x, wts["f_g_norm"][l], wts["f_w_gate"][l], wts["f_w_up"][l],
                                  wts["f_w_conv"][l], wts["f_b_conv"][l], TM_P, TN_NORM,
                                  tiles_per_seq, tail_row)
        tail = tail.reshape(bsz, tiles_per_seq, SUBLANES, D_FF)
        ffn_states.append(tail[:, tiles_per_seq - 1, SUBLANES - (FFN_CONV_WIDTH - 1):])
        x = _mm_res(act, wts["f_w_down"][l], x, TM_DOWN, TN_MM, "ffn_down")
        if l == N_A_LAYERS - 1:
            k, v, kb, vb = _kv_proj(x, wts["kv_g_norm"], wts["w_kv"], wts["g_k"], TM_P, TN_NORM)
            kb = kb.reshape(bsz, T_PAD, D_MODEL)
            vb = vb.reshape(bsz, T_PAD, D_MODEL)
    x = x.reshape(bsz, T_PAD, D_MODEL)[:, N_META:t_real]
    k = k.reshape(bsz, T_PAD, N_HEADS, HEAD_DIM)[:, :t_real]
    v = v.reshape(bsz, T_PAD, N_HEADS, HEAD_DIM)[:, :t_real]
    return x, jnp.stack(conv_states), jnp.stack(ffn_states), k, v


def _sample_trunk(x, state_conv_a, state_ffn, cache_k, cache_v, page_table, wts):
    nb = x.shape[0]
    m = SAMPLE_ROWS

    def pad_rows(a):
        return jnp.pad(a, ((0, m - nb), (0, 0)))

    x = pad_rows(x.reshape(nb, D_MODEL))
    conv_states, ffn_states = [], []
    k = v = None
    for l in range(DEPTH):
        if l < N_A_LAYERS:
            u = _glu(x, wts["a_g_norm"][l], wts["a_w_pw1"][l], m, TN_MM)[:nb]
            past = state_conv_a[l]
            conv_states.append(jnp.concatenate([past[:, 1:], u[:, None]], axis=1))
            c = _dwconv_step(past, u, wts["a_w_dw"][l], wts["a_b_dw"][l])
            x = _norm_silu_mm_res(pad_rows(c), wts["a_g_cn"][l], wts["a_w_pw2"][l], x, m, TN_MM)
        else:
            j = l - N_A_LAYERS
            q = _q_proj(x, wts["b_g_norm"][j], wts["b_w_q"][j], wts["b_g_q"][j], m, TN_MM)
            o = _sb_attn_paged(q[:nb], cache_k, cache_v, page_table, wts["b_sb"][j] * LOG2E)
            x = _mm_res(pad_rows(o).astype(BF16), wts["b_w_o"][j], x, m, TN_MM, "o_proj")
        past = state_ffn[l]
        act, gate = _ffn_gate_step(x, wts["f_g_norm"][l], wts["f_w_gate"][l], wts["f_w_up"][l],
                                   wts["f_w_conv"][l], wts["f_b_conv"][l],
                                   pad_rows(past[:, 0]), pad_rows(past[:, 1]), TN_MM)
        ffn_states.append(jnp.stack([past[:, 1], gate[:nb]], axis=1))
        x = _mm_res(act, wts["f_w_down"][l], x, m, TN_MM, "ffn_down")
        if l == N_A_LAYERS - 1:
            k, v, _, _ = _kv_proj(x, wts["kv_g_norm"], wts["w_kv"], wts["g_k"], m, TN_MM)
    y = x[:nb].reshape(nb, 1, D_MODEL)
    k = k[:nb].reshape(nb, 1, N_HEADS, HEAD_DIM)
    v = v[:nb].reshape(nb, 1, N_HEADS, HEAD_DIM)
    return y, jnp.stack(conv_states), jnp.stack(ffn_states), k, v


_MATMUL_WEIGHTS = ("a_w_pw1", "a_w_pw2", "f_w_gate", "f_w_up", "f_w_down", "w_kv", "b_w_q", "b_w_o")


def kernel(x_prompt, x_sample, state_conv_a, state_ffn_conv, cache_k, cache_v, page_table, meta_tokens, a_g_norm, a_w_pw1, a_w_dw, a_b_dw, a_g_cn, a_w_pw2, f_g_norm, f_w_gate, f_w_up, f_w_down, f_w_conv, f_b_conv, kv_g_norm, w_kv, g_k, b_g_norm, b_w_q, b_g_q, b_sb, b_w_o):
    wts = dict(a_g_norm=a_g_norm, a_w_pw1=a_w_pw1, a_w_dw=a_w_dw, a_b_dw=a_b_dw, a_g_cn=a_g_cn,
               a_w_pw2=a_w_pw2, f_g_norm=f_g_norm, f_w_gate=f_w_gate, f_w_up=f_w_up,
               f_w_down=f_w_down, f_w_conv=f_w_conv, f_b_conv=f_b_conv, kv_g_norm=kv_g_norm,
               w_kv=w_kv, g_k=g_k, b_g_norm=b_g_norm, b_w_q=b_w_q, b_g_q=b_g_q, b_sb=b_sb,
               b_w_o=b_w_o)
    for name in _MATMUL_WEIGHTS:
        wts[name] = wts[name].astype(BF16)
    bp = x_prompt.shape[0]
    meta = jnp.broadcast_to(meta_tokens[None], (bp, N_META, D_MODEL))
    xp = jnp.concatenate([meta, x_prompt], axis=1)
    y_p, conv_p, ffn_p, k_p, v_p = _prompt_trunk(xp, wts)
    y_s, conv_s, ffn_s, k_s, v_s = _sample_trunk(x_sample, state_conv_a, state_ffn_conv,
                                                 cache_k, cache_v, page_table, wts)
    return (y_p, y_s, conv_p, ffn_p, k_p, v_p, conv_s, ffn_s, k_s, v_s)
```

```python
import functools

import jax
import jax.numpy as jnp
from jax import lax
from jax.experimental import pallas as pl
from jax.experimental.pallas import tpu as pltpu

D_MODEL = 2048
D_FF = 5632
N_HEADS = 16
HEAD_DIM = 128
N_META = 16
N_A_LAYERS = 2
DEPTH = 4
CONV_A_WIDTH = 31
FFN_CONV_WIDTH = 3
PAGE_SIZE = 128
EPS = 1e-6

LANES = 128
SUBLANES = 8
VMEM_LIMIT = 56 * 1024 * 1024

F32 = jnp.float32
BF16 = jnp.bfloat16


def _cparams(sem):
    return pltpu.CompilerParams(dimension_semantics=sem, vmem_limit_bytes=VMEM_LIMIT)


def _rms_rows(x, g):
    ms = jnp.mean(x * x, axis=-1, keepdims=True)
    return x * lax.rsqrt(ms + EPS) * g


def _head_rms(t, gh):
    parts = []
    for s in range(t.shape[1] // HEAD_DIM):
        parts.append(_rms_rows(t[:, s * HEAD_DIM:(s + 1) * HEAD_DIM], gh))
    return parts


LOG2E = 1.4426950408889634

Q_SCALE = HEAD_DIM ** -0.5 * LOG2E


def _softplus2(z2):
    return jnp.maximum(z2, 0.0) + jnp.log(1.0 + jnp.exp2(-jnp.abs(z2))) * LOG2E


ROW_CHUNK = 272


def _row_chunks(tm):
    rc = ROW_CHUNK if tm % ROW_CHUNK == 0 else tm
    return [slice(r0, r0 + rc) for r0 in range(0, tm, rc)]


def _bf16_weights(w_ref, wb_ref):
    if wb_ref is None:
        return w_ref
    wb_ref[...] = w_ref[...].astype(BF16)
    return wb_ref


def _weight_outs(emit, n_w, k, n, tn):
    if not emit:
        return [], []
    spec = pl.BlockSpec((k, tn), lambda *ij: (0, ij[-1]))
    return [spec] * n_w, [jax.ShapeDtypeStruct((k, n), BF16)] * n_w


def _glu_kernel(x_ref, g_ref, wa_ref, wg_ref, u_ref, *rest, emit):
    wa_o, wg_o, h_scr = rest if emit else (None, None, *rest)

    @pl.when(pl.program_id(1) == 0)
    def _():
        h_scr[...] = _rms_rows(x_ref[...], g_ref[...]).astype(BF16)

    wa_ref, wg_ref = _bf16_weights(wa_ref, wa_o), _bf16_weights(wg_ref, wg_o)
    for rows in _row_chunks(u_ref.shape[0]):
        h = h_scr[rows, :]
        a = jnp.dot(h, wa_ref[...], preferred_element_type=F32)
        gt = jnp.dot(h, wg_ref[...], preferred_element_type=F32)
        u_ref[rows, :] = a * jax.nn.sigmoid(gt)


def _w_spec(w, tn, col0=0):
    arr, layer = w
    k = arr.shape[-2]
    if layer is None:
        return pl.BlockSpec((k, tn), lambda *ij: (0, ij[-1] + col0))
    return pl.BlockSpec((None, k, tn), lambda *ij: (layer, 0, ij[-1] + col0))


def _glu(x, g, w_a, w_g, g_col0, tm, tn):
    m = x.shape[0]
    emit = w_a[0].dtype != BF16
    w_specs, w_shapes = _weight_outs(emit, 2, D_MODEL, D_MODEL, tn)
    out = pl.pallas_call(
        functools.partial(_glu_kernel, emit=emit),
        grid=(m // tm, D_MODEL // tn),
        in_specs=[
            pl.BlockSpec((tm, D_MODEL), lambda i, j: (i, 0)),
            pl.BlockSpec((1, D_MODEL), lambda i, j: (0, 0)),
            _w_spec(w_a, tn),
            _w_spec(w_g, tn, g_col0),
        ],
        out_specs=[pl.BlockSpec((tm, tn), lambda i, j: (i, j))] + w_specs,
        out_shape=[jax.ShapeDtypeStruct((m, D_MODEL), F32)] + w_shapes,
        scratch_shapes=[pltpu.VMEM((tm, D_MODEL), BF16)],
        compiler_params=_cparams(("parallel", "arbitrary")),
        name="glu",
    )(x, g.reshape(1, D_MODEL), w_a[0], w_g[0])
    return out if emit else out[0]


DW_HALO = 32


def _dwconv_seq_kernel(u_ref, halo_ref, w_ref, b_ref, o_ref, full_scr, sh_scr, *, tt, cl):
    first = pl.program_id(1) == 0
    full_scr[0:DW_HALO, :] = jnp.where(first, 0.0, halo_ref[0])
    full_scr[DW_HALO:DW_HALO + tt, :] = u_ref[0]
    lead = DW_HALO - (CONV_A_WIDTH - 1)
    sh_rows = sh_scr.shape[1]
    for c in range(cl // LANES):
        ls = slice(c * LANES, (c + 1) * LANES)
        for s in range(1, SUBLANES):
            sh_scr[s - 1, :, ls] = full_scr[s:s + sh_rows, ls]
        acc = jnp.broadcast_to(b_ref[:, ls], (tt, LANES))
        for k in range(CONV_A_WIDTH):
            s, a0 = (lead + k) % SUBLANES, (lead + k) // SUBLANES * SUBLANES
            rows = full_scr[a0:a0 + tt, ls] if s == 0 else sh_scr[s - 1, a0:a0 + tt, ls]
            acc = acc + w_ref[k:k + 1, ls] * rows
        o_ref[0, :, ls] = acc


def _dwconv_seq(u3, w, b, tt=128, cl=1024):
    bsz, tp, _ = u3.shape
    w_pad = jnp.zeros((DW_HALO, D_MODEL), F32).at[:CONV_A_WIDTH].set(w)
    hb = tt // DW_HALO
    return pl.pallas_call(
        functools.partial(_dwconv_seq_kernel, tt=tt, cl=cl),
        grid=(bsz, tp // tt, D_MODEL // cl),
        in_specs=[
            pl.BlockSpec((1, tt, cl), lambda bb, i, c: (bb, i, c)),
            pl.BlockSpec((1, DW_HALO, cl), lambda bb, i, c: (bb, jnp.maximum(i * hb - 1, 0), c)),
            pl.BlockSpec((DW_HALO, cl), lambda bb, i, c: (0, c)),
            pl.BlockSpec((1, cl), lambda bb, i, c: (0, c)),
        ],
        out_specs=pl.BlockSpec((1, tt, cl), lambda bb, i, c: (bb, i, c)),
        out_shape=jax.ShapeDtypeStruct(u3.shape, F32),
        scratch_shapes=[pltpu.VMEM((DW_HALO + tt, cl), F32),
                        pltpu.VMEM((SUBLANES - 1, DW_HALO + tt - SUBLANES, cl), F32)],
        compiler_params=_cparams(("parallel", "parallel", "parallel")),
        name="dwconv_seq",
    )(u3, u3, w_pad, b.reshape(1, D_MODEL))


def _dwconv_step_kernel(s_ref, u_ref, w_ref, b_ref, o_ref):
    w = w_ref[...]
    past = jnp.sum(s_ref[...] * w[None, :CONV_A_WIDTH - 1, :], axis=1)
    o_ref[...] = past + u_ref[...] * w[CONV_A_WIDTH - 1:CONV_A_WIDTH, :] + b_ref[...]


def _dwconv_step(state, u, w, b):
    nb = state.shape[0]
    return pl.pallas_call(
        _dwconv_step_kernel,
        out_shape=jax.ShapeDtypeStruct((nb, D_MODEL), F32),
        compiler_params=pltpu.CompilerParams(vmem_limit_bytes=VMEM_LIMIT),
        name="dwconv_step",
    )(state, u, w, b.reshape(1, D_MODEL))


def _norm_silu_mm_res_kernel(c_ref, g_ref, w_ref, x_ref, o_ref, *rest, emit):
    w_o, h_scr = rest if emit else (None, *rest)

    @pl.when(pl.program_id(1) == 0)
    def _():
        y = _rms_rows(c_ref[...], g_ref[...])
        h_scr[...] = (y * jax.nn.sigmoid(y)).astype(BF16)

    w_ref = _bf16_weights(w_ref, w_o)
    for rows in _row_chunks(o_ref.shape[0]):
        o_ref[rows, :] = x_ref[rows, :] + jnp.dot(h_scr[rows, :], w_ref[...],
                                                  preferred_element_type=F32)


def _norm_silu_mm_res(c, g, w, xres, tm, tn):
    m = c.shape[0]
    emit = w[0].dtype != BF16
    w_specs, w_shapes = _weight_outs(emit, 1, D_MODEL, D_MODEL, tn)
    out = pl.pallas_call(
        functools.partial(_norm_silu_mm_res_kernel, emit=emit),
        grid=(m // tm, D_MODEL // tn),
        in_specs=[
            pl.BlockSpec((tm, D_MODEL), lambda i, j: (i, 0)),
            pl.BlockSpec((1, D_MODEL), lambda i, j: (0, 0)),
            _w_spec(w, tn),
            pl.BlockSpec((tm, tn), lambda i, j: (i, j)),
        ],
        out_specs=[pl.BlockSpec((tm, tn), lambda i, j: (i, j))] + w_specs,
        out_shape=[jax.ShapeDtypeStruct((m, D_MODEL), F32)] + w_shapes,
        scratch_shapes=[pltpu.VMEM((tm, D_MODEL), BF16)],
        compiler_params=_cparams(("parallel", "arbitrary")),
        name="pw2",
    )(c, g.reshape(1, D_MODEL), w[0], xres)
    return out if emit else out[0]


def _ffn_gate_seq_kernel(x_ref, xp_ref, g_ref, wg_ref, wu_ref, wc_ref, bc_ref,
                         act_ref, tail_ref, h_scr, hp_scr, gs_scr, *, tm, tiles_per_seq, tail_row):
    i = pl.program_id(0)

    @pl.when(pl.program_id(1) == 0)
    def _():
        h_scr[...] = _rms_rows(x_ref[...], g_ref[...]).astype(BF16)
        hp = _rms_rows(xp_ref[...], g_ref[...])
        hp = jnp.where(i % tiles_per_seq == 0, 0.0, hp)
        hp_scr[...] = jnp.concatenate([hp, hp], axis=0).astype(BF16)

    gprev = jnp.dot(hp_scr[...], wg_ref[...], preferred_element_type=F32)
    gs_scr[0:SUBLANES, :] = gprev[0:SUBLANES]
    for rows in _row_chunks(tm):
        r0, r1 = rows.start, rows.stop
        h = h_scr[rows, :]
        gate = jnp.dot(h, wg_ref[...], preferred_element_type=F32)
        up = jnp.dot(h, wu_ref[...], preferred_element_type=F32)
        gs_scr[SUBLANES + r0:SUBLANES + r1, :] = gate
        g1 = gs_scr[SUBLANES - 1 + r0:SUBLANES - 1 + r1, :]
        g2 = gs_scr[SUBLANES - 2 + r0:SUBLANES - 2 + r1, :]
        gc = wc_ref[0:1, :] * g2 + wc_ref[1:2, :] * g1 + wc_ref[2:3, :] * gate + bc_ref[...]
        act_ref[rows, :] = (gc * jax.nn.sigmoid(gc) * up).astype(BF16)
        if r0 <= tail_row < r1:
            tail_ref[0] = gate[tail_row - r0:tail_row - r0 + SUBLANES]


def _ffn_gate_seq(x, g, wg_bf, wu_bf, wc, bc, tm, tn, tiles_per_seq, tail_row):
    m = x.shape[0]
    nt = m // tm
    hb = tm // SUBLANES
    wc_pad = jnp.zeros((SUBLANES, D_FF), F32).at[:FFN_CONV_WIDTH].set(wc)
    return pl.pallas_call(
        functools.partial(_ffn_gate_seq_kernel, tm=tm, tiles_per_seq=tiles_per_seq, tail_row=tail_row),
        grid=(nt, D_FF // tn),
        in_specs=[
            pl.BlockSpec((tm, D_MODEL), lambda i, j: (i, 0)),
            pl.BlockSpec((SUBLANES, D_MODEL), lambda i, j: (jnp.maximum(i * hb - 1, 0), 0)),
            pl.BlockSpec((1, D_MODEL), lambda i, j: (0, 0)),
            pl.BlockSpec((D_MODEL, tn), lambda i, j: (0, j)),
            pl.BlockSpec((D_MODEL, tn), lambda i, j: (0, j)),
            pl.BlockSpec((SUBLANES, tn), lambda i, j: (0, j)),
            pl.BlockSpec((1, tn), lambda i, j: (0, j)),
        ],
        out_specs=[
            pl.BlockSpec((tm, tn), lambda i, j: (i, j)),
            pl.BlockSpec((1, SUBLANES, tn), lambda i, j: (i, 0, j)),
        ],
        out_shape=[
            jax.ShapeDtypeStruct((m, D_FF), BF16),
            jax.ShapeDtypeStruct((nt, SUBLANES, D_FF), F32),
        ],
        scratch_shapes=[
            pltpu.VMEM((tm, D_MODEL), BF16),
            pltpu.VMEM((2 * SUBLANES, D_MODEL), BF16),
            pltpu.VMEM((SUBLANES + tm, tn), F32),
        ],
        compiler_params=_cparams(("parallel", "arbitrary")),
        name="ffn_gate_seq",
    )(x, x, g.reshape(1, D_MODEL), wg_bf, wu_bf, wc_pad, bc.reshape(1, D_FF))


def _ffn_gate_step_kernel(x_ref, g_ref, wg_ref, wu_ref, wc_ref, bc_ref, s0_ref, s1_ref,
                          act_ref, gate_ref, wg_o, wu_o, h_scr):
    @pl.when(pl.program_id(0) == 0)
    def _():
        h_scr[...] = _rms_rows(x_ref[...], g_ref[...]).astype(BF16)

    h = h_scr[...]
    gate = jnp.dot(h, _bf16_weights(wg_ref, wg_o)[...], preferred_element_type=F32)
    up = jnp.dot(h, _bf16_weights(wu_ref, wu_o)[...], preferred_element_type=F32)
    gc = (wc_ref[0:1, :] * s0_ref[...] + wc_ref[1:2, :] * s1_ref[...]
          + wc_ref[2:3, :] * gate + bc_ref[...])
    act_ref[...] = (gc * jax.nn.sigmoid(gc) * up).astype(BF16)
    gate_ref[...] = gate


def _ffn_gate_step(x, g, wg, wu, wc, bc, s0, s1, tn):
    m = x.shape[0]
    wc_pad = jnp.zeros((SUBLANES, D_FF), F32).at[:FFN_CONV_WIDTH].set(wc)
    w_specs, w_shapes = _weight_outs(True, 2, D_MODEL, D_FF, tn)
    return pl.pallas_call(
        _ffn_gate_step_kernel,
        grid=(D_FF // tn,),
        in_specs=[
            pl.BlockSpec((m, D_MODEL), lambda j: (0, 0)),
            pl.BlockSpec((1, D_MODEL), lambda j: (0, 0)),
            _w_spec(wg, tn),
            _w_spec(wu, tn),
            pl.BlockSpec((SUBLANES, tn), lambda j: (0, j)),
            pl.BlockSpec((1, tn), lambda j: (0, j)),
            pl.BlockSpec((m, tn), lambda j: (0, j)),
            pl.BlockSpec((m, tn), lambda j: (0, j)),
        ],
        out_specs=[
            pl.BlockSpec((m, tn), lambda j: (0, j)),
            pl.BlockSpec((m, tn), lambda j: (0, j)),
        ] + w_specs,
        out_shape=[
            jax.ShapeDtypeStruct((m, D_FF), BF16),
            jax.ShapeDtypeStruct((m, D_FF), F32),
        ] + w_shapes,
        scratch_shapes=[pltpu.VMEM((m, D_MODEL), BF16)],
        compiler_params=_cparams(("arbitrary",)),
        name="ffn_gate_step",
    )(x, g.reshape(1, D_MODEL), wg[0], wu[0], wc_pad, bc.reshape(1, D_FF), s0, s1)


def _mm_res_kernel(a_ref, w_ref, x_ref, o_ref, w_o=None):
    w_ref = _bf16_weights(w_ref, w_o)
    for rows in _row_chunks(o_ref.shape[0]):
        o_ref[rows, :] = x_ref[rows, :] + jnp.dot(a_ref[rows, :], w_ref[...],
                                                  preferred_element_type=F32)


def _mm_res(a_bf, w, xres, tm, tn, name):
    m, kdim = a_bf.shape
    n = w[0].shape[-1]
    emit = w[0].dtype != BF16
    w_specs, w_shapes = _weight_outs(emit, 1, kdim, n, tn)
    out = pl.pallas_call(
        _mm_res_kernel,
        grid=(m // tm, n // tn),
        in_specs=[
            pl.BlockSpec((tm, kdim), lambda i, j: (i, 0)),
            _w_spec(w, tn),
            pl.BlockSpec((tm, tn), lambda i, j: (i, j)),
        ],
        out_specs=[pl.BlockSpec((tm, tn), lambda i, j: (i, j))] + w_specs,
        out_shape=[jax.ShapeDtypeStruct((m, n), F32)] + w_shapes,
        compiler_params=_cparams(("parallel", "parallel")),
        name=name,
    )(a_bf, w[0], xres)
    return out if emit else out[0]


def _kv_kernel(x_ref, g_ref, wk_ref, wv_ref, gk_ref, k_ref, v_ref, kb_ref, vb_ref, *rest, emit):
    wk_o, wv_o, h_scr = rest if emit else (None, None, *rest)

    @pl.when(pl.program_id(1) == 0)
    def _():
        h_scr[...] = _rms_rows(x_ref[...], g_ref[...]).astype(BF16)

    wk_ref, wv_ref = _bf16_weights(wk_ref, wk_o), _bf16_weights(wv_ref, wv_o)
    for rows in _row_chunks(h_scr.shape[0]):
        h = h_scr[rows, :]
        kraw = jnp.dot(h, wk_ref[...], preferred_element_type=F32)
        v = jnp.dot(h, wv_ref[...], preferred_element_type=F32)
        for s, kn in enumerate(_head_rms(kraw, gk_ref[...])):
            hc = slice(s * HEAD_DIM, (s + 1) * HEAD_DIM)
            k_ref[rows, hc] = kn
            kb_ref[rows, hc] = kn.astype(BF16)
        v_ref[rows, :] = v
        vb_ref[rows, :] = v.astype(BF16)


def _kv_proj(x, g, w_k, w_v, v_col0, gk, tm, tn, tiles_per_seq, t_out):
    m = x.shape[0]
    bsz = m // (tm * tiles_per_seq)
    emit = w_k[0].dtype != BF16
    w_specs, w_shapes = _weight_outs(emit, 2, D_MODEL, D_MODEL, tn)
    blk = pl.BlockSpec((tm, tn), lambda i, j: (i, j))
    blk_seq = pl.BlockSpec((None, tm, tn), lambda i, j: (i // tiles_per_seq, i % tiles_per_seq, j))
    return pl.pallas_call(
        functools.partial(_kv_kernel, emit=emit),
        grid=(m // tm, D_MODEL // tn),
        in_specs=[
            pl.BlockSpec((tm, D_MODEL), lambda i, j: (i, 0)),
            pl.BlockSpec((1, D_MODEL), lambda i, j: (0, 0)),
            _w_spec(w_k, tn),
            _w_spec(w_v, tn, v_col0),
            pl.BlockSpec((1, HEAD_DIM), lambda i, j: (0, 0)),
        ],
        out_specs=[blk_seq, blk_seq, blk, blk] + w_specs,
        out_shape=[
            jax.ShapeDtypeStruct((bsz, t_out, D_MODEL), F32),
            jax.ShapeDtypeStruct((bsz, t_out, D_MODEL), F32),
            jax.ShapeDtypeStruct((m, D_MODEL), BF16),
            jax.ShapeDtypeStruct((m, D_MODEL), BF16),
        ] + w_shapes,
        scratch_shapes=[pltpu.VMEM((tm, D_MODEL), BF16)],
        compiler_params=_cparams(("parallel", "arbitrary")),
        name="kv_proj",
    )(x, g.reshape(1, D_MODEL), w_k[0], w_v[0], gk.reshape(1, HEAD_DIM))


def _q_kernel(x_ref, g_ref, w_ref, gq_ref, q_ref, *rest, emit):
    w_o, h_scr = rest if emit else (None, *rest)

    @pl.when(pl.program_id(1) == 0)
    def _():
        h_scr[...] = _rms_rows(x_ref[...], g_ref[...]).astype(BF16)

    w_ref = _bf16_weights(w_ref, w_o)
    for rows in _row_chunks(h_scr.shape[0]):
        qraw = jnp.dot(h_scr[rows, :], w_ref[...], preferred_element_type=F32)
        for s, qn in enumerate(_head_rms(qraw, gq_ref[...])):
            q_ref[rows, s * HEAD_DIM:(s + 1) * HEAD_DIM] = (qn * Q_SCALE).astype(BF16)


def _q_proj(x, g, w, gq, tm, tn):
    m = x.shape[0]
    emit = w[0].dtype != BF16
    w_specs, w_shapes = _weight_outs(emit, 1, D_MODEL, D_MODEL, tn)
    out = pl.pallas_call(
        functools.partial(_q_kernel, emit=emit),
        grid=(m // tm, D_MODEL // tn),
        in_specs=[
            pl.BlockSpec((tm, D_MODEL), lambda i, j: (i, 0)),
            pl.BlockSpec((1, D_MODEL), lambda i, j: (0, 0)),
            _w_spec(w, tn),
            pl.BlockSpec((1, HEAD_DIM), lambda i, j: (0, 0)),
        ],
        out_specs=[pl.BlockSpec((tm, tn), lambda i, j: (i, j))] + w_specs,
        out_shape=[jax.ShapeDtypeStruct((m, D_MODEL), BF16)] + w_shapes,
        scratch_shapes=[pltpu.VMEM((tm, D_MODEL), BF16)],
        compiler_params=_cparams(("parallel", "arbitrary")),
        name="q_proj",
    )(x, g.reshape(1, D_MODEL), w[0], gq.reshape(1, HEAD_DIM))
    return out if emit else out[0]


NT_DIMS = (((1,), (1,)), ((), ()))


def _neg_suffix(tk):
    r = lax.broadcasted_iota(jnp.int32, (tk, tk), 0)
    c = lax.broadcasted_iota(jnp.int32, (tk, tk), 1)
    return -(r >= c).astype(BF16)


def _neg_suffix_and_total(tk):
    return jnp.concatenate([_neg_suffix(tk), -jnp.ones((tk, tk), BF16)], axis=1)


def _sb_attn_seq_kernel(sb_ref, q_ref, k_ref, v_ref, nst_ref, o_ref, acc_scr, run_scr,
                        *, tq, tk, hs):
    hg = pl.program_id(1)
    qi = pl.program_id(2)
    sub = tq // tk
    nst = nst_ref[...]
    acc_scr[...] = jnp.zeros_like(acc_scr)
    run_scr[...] = jnp.zeros_like(run_scr)
    lanes = [slice(s * HEAD_DIM, (s + 1) * HEAD_DIM) for s in range(hs)]

    def logits(j, r0):
        off = pl.multiple_of(j * tk, tk)
        return [lax.dot_general(q_ref[0, r0:tq, lanes[s]], k_ref[0, pl.ds(off, tk), lanes[s]],
                                NT_DIMS, preferred_element_type=F32) + sb_ref[hg * hs + s]
                for s in range(hs)]

    def block(j, r0, masked):
        rows = tq - r0
        off = pl.multiple_of(j * tk, tk)
        zs = logits(j, r0)
        if masked:
            diff = (lax.broadcasted_iota(jnp.int32, (rows, tk), 0)
                    - lax.broadcasted_iota(jnp.int32, (rows, tk), 1))
            mask = diff > (j * tk - qi * tq - r0)
        css = []
        for s in range(hs):
            sp = _softplus2(zs[s])
            if masked:
                sp = jnp.where(mask, sp, 0.0)
            css.append(jnp.dot(sp.astype(BF16), nst, preferred_element_type=F32))
        for s in range(hs):
            run = run_scr[s, r0:tq, :]
            att = jnp.exp2(zs[s] + css[s] + jnp.concatenate([run] * (tk // LANES), axis=1))
            if masked:
                att = jnp.where(mask, att, 0.0)
            vs = v_ref[0, pl.ds(off, tk), lanes[s]]
            acc_scr[s, r0:tq, :] += jnp.dot(att.astype(BF16), vs, preferred_element_type=F32)
            run_scr[s, r0:tq, :] = run + jnp.broadcast_to(css[s][:, 0:1], (rows, LANES))

    for d in reversed(range(sub)):
        block(qi * sub + d, d * tk, True)

    def body(jj, carry):
        block(qi * sub - 1 - jj, 0, False)
        return carry

    lax.fori_loop(0, qi * sub, body, 0)
    for s in range(hs):
        o_ref[0, :, s * HEAD_DIM:(s + 1) * HEAD_DIM] = acc_scr[s].astype(BF16)


def _sb_attn_seq(q3, kb3, vb3, sb2, tq=256, tk=256, hs=8):
    bsz, tp, _ = q3.shape
    wl = hs * HEAD_DIM
    grid_spec = pltpu.PrefetchScalarGridSpec(
        num_scalar_prefetch=1,
        grid=(bsz, N_HEADS // hs, tp // tq),
        in_specs=[
            pl.BlockSpec((1, tq, wl), lambda b, h, i, sbr: (b, i, h)),
            pl.BlockSpec((1, tp, wl), lambda b, h, i, sbr: (b, 0, h)),
            pl.BlockSpec((1, tp, wl), lambda b, h, i, sbr: (b, 0, h)),
            pl.BlockSpec((tk, tk), lambda b, h, i, sbr: (0, 0)),
        ],
        out_specs=pl.BlockSpec((1, tq, wl), lambda b, h, i, sbr: (b, i, h)),
        scratch_shapes=[pltpu.VMEM((hs, tq, HEAD_DIM), F32), pltpu.VMEM((hs, tq, LANES), F32)],
    )
    return pl.pallas_call(
        functools.partial(_sb_attn_seq_kernel, tq=tq, tk=tk, hs=hs),
        grid_spec=grid_spec,
        out_shape=jax.ShapeDtypeStruct(q3.shape, BF16),
        compiler_params=_cparams(("parallel", "parallel", "arbitrary")),
        name="sb_attn_seq",
    )(sb2, q3, kb3, vb3, _neg_suffix(tk))


PAGE_ROWS = PAGE_SIZE * N_HEADS
PAGES_PER_STEP = 4


def _sb_attn_paged_kernel(pt_ref, q_ref, bias_ref, *refs, n_steps, pp):
    k_refs, v_refs = refs[:pp], refs[pp:2 * pp]
    nst_ref, o_ref, acc_scr, run_scr = refs[2 * pp:]
    step = pl.program_id(1)

    @pl.when(step == 0)
    def _():
        acc_scr[...] = jnp.zeros_like(acc_scr)
        run_scr[...] = jnp.zeros_like(run_scr)

    n_tiles = PAGE_ROWS // LANES
    q = q_ref[0]
    bias = jnp.concatenate([bias_ref[...]] * n_tiles, axis=1)
    own = ((lax.broadcasted_iota(jnp.int32, (N_HEADS, PAGE_ROWS), 1) & (N_HEADS - 1))
           == lax.broadcasted_iota(jnp.int32, (N_HEADS, PAGE_ROWS), 0))
    nst = nst_ref[...]
    run = run_scr[...]
    acc = acc_scr[...]
    zs = []
    for i in range(pp):
        kb = k_refs[i][0].astype(BF16)
        zs.append(lax.dot_general(q, kb, NT_DIMS, preferred_element_type=F32) + bias)
    css = []
    for i in range(pp):
        sp = jnp.where(own, _softplus2(zs[i]), 0.0)
        sp_st = jnp.concatenate([sp[:, t * LANES:(t + 1) * LANES] for t in range(n_tiles)], axis=0)
        css.append(jnp.dot(sp_st.astype(BF16), nst, preferred_element_type=F32))
    for i in range(pp):
        logw = [None] * n_tiles
        for t in reversed(range(n_tiles)):
            rows = slice(t * N_HEADS, (t + 1) * N_HEADS)
            logw[t] = zs[i][:, t * LANES:(t + 1) * LANES] + css[i][rows, :LANES] + run
            run = run + css[i][rows, LANES:]
        att = jnp.where(own, jnp.exp2(jnp.concatenate(logw, axis=1)), 0.0)
        vb = v_refs[i][0].astype(BF16)
        acc = acc + jnp.dot(att.astype(BF16), vb, preferred_element_type=F32)
    acc_scr[...] = acc
    run_scr[...] = run

    @pl.when(step == n_steps - 1)
    def _():
        o_ref[0] = acc


def _sb_attn_paged(q, cache_k, cache_v, page_table, sb2):
    assert N_HEADS & (N_HEADS - 1) == 0 and LANES % N_HEADS == 0
    nb, n_pages = page_table.shape
    n_pool = cache_k.shape[0]
    pp = PAGES_PER_STEP
    n_steps = n_pages // pp
    assert n_steps * pp == n_pages
    ck = cache_k.reshape(n_pool, PAGE_ROWS, HEAD_DIM)
    cv = cache_v.reshape(n_pool, PAGE_ROWS, HEAD_DIM)
    q3 = q.reshape(nb, N_HEADS, HEAD_DIM)
    bias = jnp.broadcast_to(sb2.astype(F32)[:, None], (N_HEADS, LANES))

    def page_spec(i):
        return pl.BlockSpec((1, PAGE_ROWS, HEAD_DIM),
                            lambda b, s, pt: (pt[b, n_pages - 1 - (s * pp + i)], 0, 0))

    grid_spec = pltpu.PrefetchScalarGridSpec(
        num_scalar_prefetch=1,
        grid=(nb, n_steps),
        in_specs=[
            pl.BlockSpec((1, N_HEADS, HEAD_DIM), lambda b, s, pt: (b, 0, 0)),
            pl.BlockSpec((N_HEADS, LANES), lambda b, s, pt: (0, 0)),
            *[page_spec(i) for i in range(pp)],
            *[page_spec(i) for i in range(pp)],
            pl.BlockSpec((LANES, 2 * LANES), lambda b, s, pt: (0, 0)),
        ],
        out_specs=pl.BlockSpec((1, N_HEADS, HEAD_DIM), lambda b, s, pt: (b, 0, 0)),
        scratch_shapes=[pltpu.VMEM((N_HEADS, HEAD_DIM), F32), pltpu.VMEM((N_HEADS, LANES), F32)],
    )
    o = pl.pallas_call(
        functools.partial(_sb_attn_paged_kernel, n_steps=n_steps, pp=pp),
        grid_spec=grid_spec,
        out_shape=jax.ShapeDtypeStruct((nb, N_HEADS, HEAD_DIM), F32),
        compiler_params=_cparams(("parallel", "arbitrary")),
        name="sb_attn_paged",
    )(page_table, q3, bias, *([ck] * pp), *([cv] * pp), _neg_suffix_and_total(LANES))
    return o.reshape(nb, D_MODEL)


T_PAD = 4352
TM_P = 1088
TM_DOWN = 1088
TN_NORM = 512
TN_MM = 512
SAMPLE_ROWS = 16


def _prompt_trunk(x, wts, wb):
    bsz, t_real, _ = x.shape
    pad = T_PAD - t_real
    x = jnp.pad(x, ((0, 0), (0, pad), (0, 0))).reshape(bsz * T_PAD, D_MODEL)
    tiles_per_seq = T_PAD // TM_P
    tail_row = (t_real - 1) % TM_P - (SUBLANES - 1)
    assert (t_real - 1) // TM_P == tiles_per_seq - 1 and tail_row % SUBLANES == 0
    conv_states, ffn_states = [], []
    k = v = kb = vb = None
    for l in range(DEPTH):
        if l < N_A_LAYERS:
            u = _glu(x, wts["a_g_norm"][l], (wb["pw1_a"][l], None), (wb["pw1_g"][l], None), 0,
                     TM_P, TN_NORM)
            u3 = u.reshape(bsz, T_PAD, D_MODEL)
            conv_states.append(u3[:, t_real - (CONV_A_WIDTH - 1):t_real])
            c = _dwconv_seq(u3, wts["a_w_dw"][l], wts["a_b_dw"][l])
            x = _norm_silu_mm_res(c.reshape(bsz * T_PAD, D_MODEL), wts["a_g_cn"][l],
                                  (wb["pw2"][l], None), x, TM_P, TN_NORM)
        else:
            j = l - N_A_LAYERS
            q = _q_proj(x, wts["b_g_norm"][j], (wb["q"][j], None), wts["b_g_q"][j], TM_P, TN_NORM)
            o = _sb_attn_seq(q.reshape(bsz, T_PAD, D_MODEL), kb, vb, wts["b_sb"][j] * LOG2E)
            x = _mm_res(o.reshape(bsz * T_PAD, D_MODEL), (wb["o"][j], None), x, TM_P, TN_MM,
                        "o_proj")
        act, tail = _ffn_gate_seq(x, wts["f_g_norm"][l], wb["gate"][l], wb["up"][l],
                                  wts["f_w_conv"][l], wts["f_b_conv"][l], TM_P, TN_NORM,
                                  tiles_per_seq, tail_row)
        tail = tail.reshape(bsz, tiles_per_seq, SUBLANES, D_FF)
        ffn_states.append(tail[:, tiles_per_seq - 1, SUBLANES - (FFN_CONV_WIDTH - 1):])
        x = _mm_res(act, (wb["down"][l], None), x, TM_DOWN, TN_MM, "ffn_down")
        if l == N_A_LAYERS - 1:
            k, v, kb, vb = _kv_proj(x, wts["kv_g_norm"], (wb["kv_k"], None), (wb["kv_v"], None), 0,
                                    wts["g_k"], TM_P, TN_NORM, tiles_per_seq, t_real)
            kb = kb.reshape(bsz, T_PAD, D_MODEL)
            vb = vb.reshape(bsz, T_PAD, D_MODEL)
    x = x.reshape(bsz, T_PAD, D_MODEL)[:, N_META:t_real]
    k = k.reshape(bsz, t_real, N_HEADS, HEAD_DIM)
    v = v.reshape(bsz, t_real, N_HEADS, HEAD_DIM)
    return x, jnp.stack(conv_states), jnp.stack(ffn_states), k, v


def _sample_trunk(x, state_conv_a, state_ffn, cache_k, cache_v, page_table, wts):
    nb = x.shape[0]
    m = SAMPLE_ROWS

    def pad_rows(a):
        return jnp.pad(a, ((0, m - nb), (0, 0)))

    x = pad_rows(x.reshape(nb, D_MODEL))
    conv_states, ffn_states = [], []
    k = v = None
    wb = {name: [] for name in ("pw1_a", "pw1_g", "pw2", "gate", "up", "down", "q", "o")}
    half = D_MODEL // TN_MM
    for l in range(DEPTH):
        if l < N_A_LAYERS:
            w1 = (wts["a_w_pw1"], l)
            u, wa, wg = _glu(x, wts["a_g_norm"][l], w1, w1, half, m, TN_MM)
            wb["pw1_a"].append(wa)
            wb["pw1_g"].append(wg)
            u = u[:nb]
            past = state_conv_a[l]
            conv_states.append(jnp.concatenate([past[:, 1:], u[:, None]], axis=1))
            c = _dwconv_step(past, u, wts["a_w_dw"][l], wts["a_b_dw"][l])
            x, w2 = _norm_silu_mm_res(pad_rows(c), wts["a_g_cn"][l], (wts["a_w_pw2"], l), x, m, TN_MM)
            wb["pw2"].append(w2)
        else:
            j = l - N_A_LAYERS
            q, wq = _q_proj(x, wts["b_g_norm"][j], (wts["b_w_q"], j), wts["b_g_q"][j], m, TN_MM)
            wb["q"].append(wq)
            o = _sb_attn_paged(q[:nb], cache_k, cache_v, page_table, wts["b_sb"][j] * LOG2E)
            x, wo = _mm_res(pad_rows(o).astype(BF16), (wts["b_w_o"], j), x, m, TN_MM, "o_proj")
            wb["o"].append(wo)
        past = state_ffn[l]
        act, gate, wgate, wup = _ffn_gate_step(
            x, wts["f_g_norm"][l], (wts["f_w_gate"], l), (wts["f_w_up"], l), wts["f_w_conv"][l],
            wts["f_b_conv"][l], pad_rows(past[:, 0]), pad_rows(past[:, 1]), TN_MM)
        wb["gate"].append(wgate)
        wb["up"].append(wup)
        ffn_states.append(jnp.stack([past[:, 1], gate[:nb]], axis=1))
        x, wdown = _mm_res(act, (wts["f_w_down"], l), x, m, TN_MM, "ffn_down")
        wb["down"].append(wdown)
        if l == N_A_LAYERS - 1:
            wkv = (wts["w_kv"], None)
            k, v, _, _, wb["kv_k"], wb["kv_v"] = _kv_proj(x, wts["kv_g_norm"], wkv, wkv, half,
                                                          wts["g_k"], m, TN_MM, 1, nb)
    y = x[:nb].reshape(nb, 1, D_MODEL)
    k = k.reshape(nb, 1, N_HEADS, HEAD_DIM)
    v = v.reshape(nb, 1, N_HEADS, HEAD_DIM)
    return (y, jnp.stack(conv_states), jnp.stack(ffn_states), k, v), wb


def kernel(x_prompt, x_sample, state_conv_a, state_ffn_conv, cache_k, cache_v, page_table, meta_tokens, a_g_norm, a_w_pw1, a_w_dw, a_b_dw, a_g_cn, a_w_pw2, f_g_norm, f_w_gate, f_w_up, f_w_down, f_w_conv, f_b_conv, kv_g_norm, w_kv, g_k, b_g_norm, b_w_q, b_g_q, b_sb, b_w_o):
    wts = dict(a_g_norm=a_g_norm, a_w_pw1=a_w_pw1, a_w_dw=a_w_dw, a_b_dw=a_b_dw, a_g_cn=a_g_cn,
               a_w_pw2=a_w_pw2, f_g_norm=f_g_norm, f_w_gate=f_w_gate, f_w_up=f_w_up,
               f_w_down=f_w_down, f_w_conv=f_w_conv, f_b_conv=f_b_conv, kv_g_norm=kv_g_norm,
               w_kv=w_kv, g_k=g_k, b_g_norm=b_g_norm, b_w_q=b_w_q, b_g_q=b_g_q, b_sb=b_sb,
               b_w_o=b_w_o)
    (y_s, conv_s, ffn_s, k_s, v_s), wb = _sample_trunk(x_sample, state_conv_a, state_ffn_conv,
                                                       cache_k, cache_v, page_table, wts)
    bp = x_prompt.shape[0]
    meta = jnp.broadcast_to(meta_tokens[None], (bp, N_META, D_MODEL))
    xp = jnp.concatenate([meta, x_prompt], axis=1)
    y_p, conv_p, ffn_p, k_p, v_p = _prompt_trunk(xp, wts, wb)
    return (y_p, y_s, conv_p, ffn_p, k_p, v_p, conv_s, ffn_s, k_s, v_s)
```

```python
import functools

import jax
import jax.numpy as jnp
from jax import lax
from jax.experimental import pallas as pl
from jax.experimental.pallas import tpu as pltpu

D_MODEL = 2048
D_FF = 5632
N_HEADS = 16
HEAD_DIM = 128
N_META = 16
N_A_LAYERS = 2
DEPTH = 4
CONV_A_WIDTH = 31
FFN_CONV_WIDTH = 3
PAGE_SIZE = 128
EPS = 1e-6

LANES = 128
SUBLANES = 8
VMEM_LIMIT = 56 * 1024 * 1024

F32 = jnp.float32
BF16 = jnp.bfloat16


def _cparams(sem):
    return pltpu.CompilerParams(dimension_semantics=sem, vmem_limit_bytes=VMEM_LIMIT)


def _rms_rows(x, g):
    ms = jnp.mean(x * x, axis=-1, keepdims=True)
    return x * lax.rsqrt(ms + EPS) * g


def _head_rms(t, gh):
    parts = []
    for s in range(t.shape[1] // HEAD_DIM):
        parts.append(_rms_rows(t[:, s * HEAD_DIM:(s + 1) * HEAD_DIM], gh))
    return parts


LOG2E = 1.4426950408889634

Q_SCALE = HEAD_DIM ** -0.5 * LOG2E


def _softplus2(z2):
    return jnp.maximum(z2, 0.0) + jnp.log(1.0 + jnp.exp2(-jnp.abs(z2))) * LOG2E


ROW_CHUNK = 272


def _row_chunks(tm):
    rc = ROW_CHUNK if tm % ROW_CHUNK == 0 else tm
    return [slice(r0, r0 + rc) for r0 in range(0, tm, rc)]


def _bf16_weights(w_ref, wb_ref):
    if wb_ref is None:
        return w_ref
    wb_ref[...] = w_ref[...].astype(BF16)
    return wb_ref


def _weight_outs(emit, n_w, k, n, tn):
    if not emit:
        return [], []
    spec = pl.BlockSpec((k, tn), lambda *ij: (0, ij[-1]))
    return [spec] * n_w, [jax.ShapeDtypeStruct((k, n), BF16)] * n_w


def _staged_chunks(h_scr, stage, consume):
    chunks = _row_chunks(h_scr.shape[0])

    @pl.when(pl.program_id(1) == 0)
    def _():
        for rows in chunks:
            h_scr[rows, :] = stage(rows)
            consume(rows)

    @pl.when(pl.program_id(1) != 0)
    def _():
        for rows in chunks:
            consume(rows)


def _glu_kernel(x_ref, g_ref, wa_ref, wg_ref, u_ref, *rest, emit):
    wa_o, wg_o, h_scr = rest if emit else (None, None, *rest)
    wa_ref, wg_ref = _bf16_weights(wa_ref, wa_o), _bf16_weights(wg_ref, wg_o)

    def consume(rows):
        h = h_scr[rows, :]
        a = jnp.dot(h, wa_ref[...], preferred_element_type=F32)
        gt = jnp.dot(h, wg_ref[...], preferred_element_type=F32)
        u_ref[rows, :] = a * jax.nn.sigmoid(gt)

    _staged_chunks(h_scr, lambda rows: _rms_rows(x_ref[rows, :], g_ref[...]).astype(BF16), consume)


def _w_spec(w, tn, col0=0):
    arr, layer = w
    k = arr.shape[-2]
    if layer is None:
        return pl.BlockSpec((k, tn), lambda *ij: (0, ij[-1] + col0))
    return pl.BlockSpec((None, k, tn), lambda *ij: (layer, 0, ij[-1] + col0))


def _glu(x, g, w_a, w_g, g_col0, tm, tn):
    m = x.shape[0]
    emit = w_a[0].dtype != BF16
    w_specs, w_shapes = _weight_outs(emit, 2, D_MODEL, D_MODEL, tn)
    out = pl.pallas_call(
        functools.partial(_glu_kernel, emit=emit),
        grid=(m // tm, D_MODEL // tn),
        in_specs=[
            pl.BlockSpec((tm, D_MODEL), lambda i, j: (i, 0)),
            pl.BlockSpec((1, D_MODEL), lambda i, j: (0, 0)),
            _w_spec(w_a, tn),
            _w_spec(w_g, tn, g_col0),
        ],
        out_specs=[pl.BlockSpec((tm, tn), lambda i, j: (i, j))] + w_specs,
        out_shape=[jax.ShapeDtypeStruct((m, D_MODEL), F32)] + w_shapes,
        scratch_shapes=[pltpu.VMEM((tm, D_MODEL), BF16)],
        compiler_params=_cparams(("parallel", "arbitrary")),
        name="glu",
    )(x, g.reshape(1, D_MODEL), w_a[0], w_g[0])
    return out if emit else out[0]


DW_HALO = 32


def _dwconv_seq_kernel(u_ref, halo_ref, w_ref, b_ref, o_ref, full_scr, sh_scr, *, tt, cl):
    first = pl.program_id(1) == 0
    full_scr[0:DW_HALO, :] = jnp.where(first, 0.0, halo_ref[0])
    full_scr[DW_HALO:DW_HALO + tt, :] = u_ref[0]
    lead = DW_HALO - (CONV_A_WIDTH - 1)
    sh_rows = sh_scr.shape[1]
    for c in range(cl // LANES):
        ls = slice(c * LANES, (c + 1) * LANES)
        for s in range(1, SUBLANES):
            sh_scr[s - 1, :, ls] = full_scr[s:s + sh_rows, ls]
        acc = jnp.broadcast_to(b_ref[:, ls], (tt, LANES))
        for k in range(CONV_A_WIDTH):
            s, a0 = (lead + k) % SUBLANES, (lead + k) // SUBLANES * SUBLANES
            rows = full_scr[a0:a0 + tt, ls] if s == 0 else sh_scr[s - 1, a0:a0 + tt, ls]
            acc = acc + w_ref[k:k + 1, ls] * rows
        o_ref[0, :, ls] = acc


def _dwconv_seq(u3, w, b, tt=128, cl=1024):
    bsz, tp, _ = u3.shape
    w_pad = jnp.zeros((DW_HALO, D_MODEL), F32).at[:CONV_A_WIDTH].set(w)
    hb = tt // DW_HALO
    return pl.pallas_call(
        functools.partial(_dwconv_seq_kernel, tt=tt, cl=cl),
        grid=(bsz, tp // tt, D_MODEL // cl),
        in_specs=[
            pl.BlockSpec((1, tt, cl), lambda bb, i, c: (bb, i, c)),
            pl.BlockSpec((1, DW_HALO, cl), lambda bb, i, c: (bb, jnp.maximum(i * hb - 1, 0), c)),
            pl.BlockSpec((DW_HALO, cl), lambda bb, i, c: (0, c)),
            pl.BlockSpec((1, cl), lambda bb, i, c: (0, c)),
        ],
        out_specs=pl.BlockSpec((1, tt, cl), lambda bb, i, c: (bb, i, c)),
        out_shape=jax.ShapeDtypeStruct(u3.shape, F32),
        scratch_shapes=[pltpu.VMEM((DW_HALO + tt, cl), F32),
                        pltpu.VMEM((SUBLANES - 1, DW_HALO + tt - SUBLANES, cl), F32)],
        compiler_params=_cparams(("parallel", "parallel", "parallel")),
        name="dwconv_seq",
    )(u3, u3, w_pad, b.reshape(1, D_MODEL))


def _dwconv_step_kernel(s_ref, u_ref, w_ref, b_ref, o_ref):
    w = w_ref[...]
    past = jnp.sum(s_ref[...] * w[None, :CONV_A_WIDTH - 1, :], axis=1)
    o_ref[...] = past + u_ref[...] * w[CONV_A_WIDTH - 1:CONV_A_WIDTH, :] + b_ref[...]


def _dwconv_step(state, u, w, b):
    nb = state.shape[0]
    return pl.pallas_call(
        _dwconv_step_kernel,
        out_shape=jax.ShapeDtypeStruct((nb, D_MODEL), F32),
        compiler_params=pltpu.CompilerParams(vmem_limit_bytes=VMEM_LIMIT),
        name="dwconv_step",
    )(state, u, w, b.reshape(1, D_MODEL))


def _norm_silu_mm_res_kernel(c_ref, g_ref, w_ref, x_ref, o_ref, *rest, emit):
    w_o, h_scr = rest if emit else (None, *rest)
    w_ref = _bf16_weights(w_ref, w_o)

    def stage(rows):
        y = _rms_rows(c_ref[rows, :], g_ref[...])
        return (y * jax.nn.sigmoid(y)).astype(BF16)

    def consume(rows):
        o_ref[rows, :] = x_ref[rows, :] + jnp.dot(h_scr[rows, :], w_ref[...],
                                                  preferred_element_type=F32)

    _staged_chunks(h_scr, stage, consume)


def _norm_silu_mm_res(c, g, w, xres, tm, tn):
    m = c.shape[0]
    emit = w[0].dtype != BF16
    w_specs, w_shapes = _weight_outs(emit, 1, D_MODEL, D_MODEL, tn)
    out = pl.pallas_call(
        functools.partial(_norm_silu_mm_res_kernel, emit=emit),
        grid=(m // tm, D_MODEL // tn),
        in_specs=[
            pl.BlockSpec((tm, D_MODEL), lambda i, j: (i, 0)),
            pl.BlockSpec((1, D_MODEL), lambda i, j: (0, 0)),
            _w_spec(w, tn),
            pl.BlockSpec((tm, tn), lambda i, j: (i, j)),
        ],
        out_specs=[pl.BlockSpec((tm, tn), lambda i, j: (i, j))] + w_specs,
        out_shape=[jax.ShapeDtypeStruct((m, D_MODEL), F32)] + w_shapes,
        scratch_shapes=[pltpu.VMEM((tm, D_MODEL), BF16)],
        compiler_params=_cparams(("parallel", "arbitrary")),
        name="pw2",
    )(c, g.reshape(1, D_MODEL), w[0], xres)
    return out if emit else out[0]


def _ffn_gate_seq_kernel(x_ref, xp_ref, g_ref, wg_ref, wu_ref, wc_ref, bc_ref,
                         act_ref, tail_ref, h_scr, hp_scr, gs_scr, *, tm, tiles_per_seq, tail_row):
    i = pl.program_id(0)

    @pl.when(pl.program_id(1) == 0)
    def _():
        hp = _rms_rows(xp_ref[...], g_ref[...])
        hp = jnp.where(i % tiles_per_seq == 0, 0.0, hp)
        hp_scr[...] = jnp.concatenate([hp, hp], axis=0).astype(BF16)

    gprev = jnp.dot(hp_scr[...], wg_ref[...], preferred_element_type=F32)
    gs_scr[0:SUBLANES, :] = gprev[0:SUBLANES]

    def consume(rows):
        r0, r1 = rows.start, rows.stop
        h = h_scr[rows, :]
        gate = jnp.dot(h, wg_ref[...], preferred_element_type=F32)
        up = jnp.dot(h, wu_ref[...], preferred_element_type=F32)
        gs_scr[SUBLANES + r0:SUBLANES + r1, :] = gate
        g1 = gs_scr[SUBLANES - 1 + r0:SUBLANES - 1 + r1, :]
        g2 = gs_scr[SUBLANES - 2 + r0:SUBLANES - 2 + r1, :]
        gc = wc_ref[0:1, :] * g2 + wc_ref[1:2, :] * g1 + wc_ref[2:3, :] * gate + bc_ref[...]
        act_ref[rows, :] = (gc * jax.nn.sigmoid(gc) * up).astype(BF16)
        if r0 <= tail_row < r1:
            tail_ref[0] = gate[tail_row - r0:tail_row - r0 + SUBLANES]

    _staged_chunks(h_scr, lambda rows: _rms_rows(x_ref[rows, :], g_ref[...]).astype(BF16), consume)


def _ffn_gate_seq(x, g, wg_bf, wu_bf, wc, bc, tm, tn, tiles_per_seq, tail_row):
    m = x.shape[0]
    nt = m // tm
    hb = tm // SUBLANES
    wc_pad = jnp.zeros((SUBLANES, D_FF), F32).at[:FFN_CONV_WIDTH].set(wc)
    return pl.pallas_call(
        functools.partial(_ffn_gate_seq_kernel, tm=tm, tiles_per_seq=tiles_per_seq, tail_row=tail_row),
        grid=(nt, D_FF // tn),
        in_specs=[
            pl.BlockSpec((tm, D_MODEL), lambda i, j: (i, 0)),
            pl.BlockSpec((SUBLANES, D_MODEL), lambda i, j: (jnp.maximum(i * hb - 1, 0), 0)),
            pl.BlockSpec((1, D_MODEL), lambda i, j: (0, 0)),
            pl.BlockSpec((D_MODEL, tn), lambda i, j: (0, j)),
            pl.BlockSpec((D_MODEL, tn), lambda i, j: (0, j)),
            pl.BlockSpec((SUBLANES, tn), lambda i, j: (0, j)),
            pl.BlockSpec((1, tn), lambda i, j: (0, j)),
        ],
        out_specs=[
            pl.BlockSpec((tm, tn), lambda i, j: (i, j)),
            pl.BlockSpec((1, SUBLANES, tn), lambda i, j: (i, 0, j)),
        ],
        out_shape=[
            jax.ShapeDtypeStruct((m, D_FF), BF16),
            jax.ShapeDtypeStruct((nt, SUBLANES, D_FF), F32),
        ],
        scratch_shapes=[
            pltpu.VMEM((tm, D_MODEL), BF16),
            pltpu.VMEM((2 * SUBLANES, D_MODEL), BF16),
            pltpu.VMEM((SUBLANES + tm, tn), F32),
        ],
        compiler_params=_cparams(("parallel", "arbitrary")),
        name="ffn_gate_seq",
    )(x, x, g.reshape(1, D_MODEL), wg_bf, wu_bf, wc_pad, bc.reshape(1, D_FF))


def _ffn_gate_step_kernel(x_ref, g_ref, wg_ref, wu_ref, wc_ref, bc_ref, s0_ref, s1_ref,
                          act_ref, gate_ref, wg_o, wu_o, h_scr):
    @pl.when(pl.program_id(0) == 0)
    def _():
        h_scr[...] = _rms_rows(x_ref[...], g_ref[...]).astype(BF16)

    h = h_scr[...]
    gate = jnp.dot(h, _bf16_weights(wg_ref, wg_o)[...], preferred_element_type=F32)
    up = jnp.dot(h, _bf16_weights(wu_ref, wu_o)[...], preferred_element_type=F32)
    gc = (wc_ref[0:1, :] * s0_ref[...] + wc_ref[1:2, :] * s1_ref[...]
          + wc_ref[2:3, :] * gate + bc_ref[...])
    act_ref[...] = (gc * jax.nn.sigmoid(gc) * up).astype(BF16)
    gate_ref[...] = gate


def _ffn_gate_step(x, g, wg, wu, wc, bc, s0, s1, tn):
    m = x.shape[0]
    wc_pad = jnp.zeros((SUBLANES, D_FF), F32).at[:FFN_CONV_WIDTH].set(wc)
    w_specs, w_shapes = _weight_outs(True, 2, D_MODEL, D_FF, tn)
    return pl.pallas_call(
        _ffn_gate_step_kernel,
        grid=(D_FF // tn,),
        in_specs=[
            pl.BlockSpec((m, D_MODEL), lambda j: (0, 0)),
            pl.BlockSpec((1, D_MODEL), lambda j: (0, 0)),
            _w_spec(wg, tn),
            _w_spec(wu, tn),
            pl.BlockSpec((SUBLANES, tn), lambda j: (0, j)),
            pl.BlockSpec((1, tn), lambda j: (0, j)),
            pl.BlockSpec((m, tn), lambda j: (0, j)),
            pl.BlockSpec((m, tn), lambda j: (0, j)),
        ],
        out_specs=[
            pl.BlockSpec((m, tn), lambda j: (0, j)),
            pl.BlockSpec((m, tn), lambda j: (0, j)),
        ] + w_specs,
        out_shape=[
            jax.ShapeDtypeStruct((m, D_FF), BF16),
            jax.ShapeDtypeStruct((m, D_FF), F32),
        ] + w_shapes,
        scratch_shapes=[pltpu.VMEM((m, D_MODEL), BF16)],
        compiler_params=_cparams(("arbitrary",)),
        name="ffn_gate_step",
    )(x, g.reshape(1, D_MODEL), wg[0], wu[0], wc_pad, bc.reshape(1, D_FF), s0, s1)


def _mm_res_kernel(a_ref, w_ref, x_ref, o_ref, w_o=None):
    w_ref = _bf16_weights(w_ref, w_o)
    for rows in _row_chunks(o_ref.shape[0]):
        o_ref[rows, :] = x_ref[rows, :] + jnp.dot(a_ref[rows, :], w_ref[...],
                                                  preferred_element_type=F32)


def _mm_res(a_bf, w, xres, tm, tn, name):
    m, kdim = a_bf.shape
    n = w[0].shape[-1]
    emit = w[0].dtype != BF16
    w_specs, w_shapes = _weight_outs(emit, 1, kdim, n, tn)
    out = pl.pallas_call(
        _mm_res_kernel,
        grid=(m // tm, n // tn),
        in_specs=[
            pl.BlockSpec((tm, kdim), lambda i, j: (i, 0)),
            _w_spec(w, tn),
            pl.BlockSpec((tm, tn), lambda i, j: (i, j)),
        ],
        out_specs=[pl.BlockSpec((tm, tn), lambda i, j: (i, j))] + w_specs,
        out_shape=[jax.ShapeDtypeStruct((m, n), F32)] + w_shapes,
        compiler_params=_cparams(("parallel", "parallel")),
        name=name,
    )(a_bf, w[0], xres)
    return out if emit else out[0]


def _kv_kernel(x_ref, g_ref, wk_ref, wv_ref, gk_ref, k_ref, v_ref, kb_ref, vb_ref, *rest, emit):
    wk_o, wv_o, h_scr = rest if emit else (None, None, *rest)
    wk_ref, wv_ref = _bf16_weights(wk_ref, wk_o), _bf16_weights(wv_ref, wv_o)

    def consume(rows):
        h = h_scr[rows, :]
        kraw = jnp.dot(h, wk_ref[...], preferred_element_type=F32)
        v = jnp.dot(h, wv_ref[...], preferred_element_type=F32)
        for s, kn in enumerate(_head_rms(kraw, gk_ref[...])):
            hc = slice(s * HEAD_DIM, (s + 1) * HEAD_DIM)
            k_ref[rows, hc] = kn
            kb_ref[rows, hc] = kn.astype(BF16)
        v_ref[rows, :] = v
        vb_ref[rows, :] = v.astype(BF16)

    _staged_chunks(h_scr, lambda rows: _rms_rows(x_ref[rows, :], g_ref[...]).astype(BF16), consume)


def _kv_proj(x, g, w_k, w_v, v_col0, gk, tm, tn, tiles_per_seq, t_out):
    m = x.shape[0]
    bsz = m // (tm * tiles_per_seq)
    emit = w_k[0].dtype != BF16
    w_specs, w_shapes = _weight_outs(emit, 2, D_MODEL, D_MODEL, tn)
    blk = pl.BlockSpec((tm, tn), lambda i, j: (i, j))
    blk_seq = pl.BlockSpec((None, tm, tn), lambda i, j: (i // tiles_per_seq, i % tiles_per_seq, j))
    return pl.pallas_call(
        functools.partial(_kv_kernel, emit=emit),
        grid=(m // tm, D_MODEL // tn),
        in_specs=[
            pl.BlockSpec((tm, D_MODEL), lambda i, j: (i, 0)),
            pl.BlockSpec((1, D_MODEL), lambda i, j: (0, 0)),
            _w_spec(w_k, tn),
            _w_spec(w_v, tn, v_col0),
            pl.BlockSpec((1, HEAD_DIM), lambda i, j: (0, 0)),
        ],
        out_specs=[blk_seq, blk_seq, blk, blk] + w_specs,
        out_shape=[
            jax.ShapeDtypeStruct((bsz, t_out, D_MODEL), F32),
            jax.ShapeDtypeStruct((bsz, t_out, D_MODEL), F32),
            jax.ShapeDtypeStruct((m, D_MODEL), BF16),
            jax.ShapeDtypeStruct((m, D_MODEL), BF16),
        ] + w_shapes,
        scratch_shapes=[pltpu.VMEM((tm, D_MODEL), BF16)],
        compiler_params=_cparams(("parallel", "arbitrary")),
        name="kv_proj",
    )(x, g.reshape(1, D_MODEL), w_k[0], w_v[0], gk.reshape(1, HEAD_DIM))


def _q_kernel(x_ref, g_ref, w_ref, gq_ref, q_ref, *rest, emit):
    w_o, h_scr = rest if emit else (None, *rest)
    w_ref = _bf16_weights(w_ref, w_o)

    def consume(rows):
        qraw = jnp.dot(h_scr[rows, :], w_ref[...], preferred_element_type=F32)
        for s, qn in enumerate(_head_rms(qraw, gq_ref[...])):
            q_ref[rows, s * HEAD_DIM:(s + 1) * HEAD_DIM] = (qn * Q_SCALE).astype(BF16)

    _staged_chunks(h_scr, lambda rows: _rms_rows(x_ref[rows, :], g_ref[...]).astype(BF16), consume)


def _q_proj(x, g, w, gq, tm, tn):
    m = x.shape[0]
    emit = w[0].dtype != BF16
    w_specs, w_shapes = _weight_outs(emit, 1, D_MODEL, D_MODEL, tn)
    out = pl.pallas_call(
        functools.partial(_q_kernel, emit=emit),
        grid=(m // tm, D_MODEL // tn),
        in_specs=[
            pl.BlockSpec((tm, D_MODEL), lambda i, j: (i, 0)),
            pl.BlockSpec((1, D_MODEL), lambda i, j: (0, 0)),
            _w_spec(w, tn),
            pl.BlockSpec((1, HEAD_DIM), lambda i, j: (0, 0)),
        ],
        out_specs=[pl.BlockSpec((tm, tn), lambda i, j: (i, j))] + w_specs,
        out_shape=[jax.ShapeDtypeStruct((m, D_MODEL), BF16)] + w_shapes,
        scratch_shapes=[pltpu.VMEM((tm, D_MODEL), BF16)],
        compiler_params=_cparams(("parallel", "arbitrary")),
        name="q_proj",
    )(x, g.reshape(1, D_MODEL), w[0], gq.reshape(1, HEAD_DIM))
    return out if emit else out[0]


NT_DIMS = (((1,), (1,)), ((), ()))


def _neg_suffix(tk):
    r = lax.broadcasted_iota(jnp.int32, (tk, tk), 0)
    c = lax.broadcasted_iota(jnp.int32, (tk, tk), 1)
    return -(r >= c).astype(BF16)


def _neg_suffix_and_total(tk):
    return jnp.concatenate([_neg_suffix(tk), -jnp.ones((tk, tk), BF16)], axis=1)


def _sb_attn_seq_kernel(sb_ref, q_ref, k_ref, v_ref, nst_ref, o_ref, acc_scr, run_scr,
                        *, tq, tk, hs, n_last):
    hg = pl.program_id(1)
    qi = pl.program_id(2)
    sub = tq // tk
    nst = nst_ref[...]
    lanes = [slice(s * HEAD_DIM, (s + 1) * HEAD_DIM) for s in range(hs)]

    def logits(j, r0, nr):
        off = pl.multiple_of(j * tk, tk)
        return [lax.dot_general(q_ref[0, r0:nr, lanes[s]], k_ref[0, pl.ds(off, tk), lanes[s]],
                                NT_DIMS, preferred_element_type=F32) + sb_ref[hg * hs + s]
                for s in range(hs)]

    def block(j, r0, nr, masked):
        rows = nr - r0
        off = pl.multiple_of(j * tk, tk)
        zs = logits(j, r0, nr)
        if masked:
            diff = (lax.broadcasted_iota(jnp.int32, (rows, tk), 0)
                    - lax.broadcasted_iota(jnp.int32, (rows, tk), 1))
            mask = diff > (j * tk - qi * tq - r0)
        css = []
        for s in range(hs):
            sp = _softplus2(zs[s])
            if masked:
                sp = jnp.where(mask, sp, 0.0)
            css.append(jnp.dot(sp.astype(BF16), nst, preferred_element_type=F32))
        for s in range(hs):
            run = run_scr[s, r0:nr, :]
            att = jnp.exp2(zs[s] + css[s] + jnp.concatenate([run] * (tk // LANES), axis=1))
            if masked:
                att = jnp.where(mask, att, 0.0)
            vs = v_ref[0, pl.ds(off, tk), lanes[s]]
            acc_scr[s, r0:nr, :] += jnp.dot(att.astype(BF16), vs, preferred_element_type=F32)
            run_scr[s, r0:nr, :] = run + jnp.broadcast_to(css[s][:, 0:1], (rows, LANES))

    def tile(nr):
        acc_scr[:, 0:nr, :] = jnp.zeros((hs, nr, HEAD_DIM), F32)
        run_scr[:, 0:nr, :] = jnp.zeros((hs, nr, LANES), F32)
        for d in reversed(range(sub)):
            if d * tk < nr:
                block(qi * sub + d, d * tk, nr, True)

        def body(jj, carry):
            block(qi * sub - 1 - jj, 0, nr, False)
            return carry

        lax.fori_loop(0, qi * sub, body, 0)
        for s in range(hs):
            o_ref[0, 0:nr, lanes[s]] = acc_scr[s, 0:nr, :].astype(BF16)
        if nr < tq:
            o_ref[0, nr:tq, :] = jnp.zeros((tq - nr, hs * HEAD_DIM), BF16)

    if n_last == tq:
        tile(tq)
    else:
        last = pl.num_programs(2) - 1
        pl.when(qi != last)(lambda: tile(tq))
        pl.when(qi == last)(lambda: tile(n_last))


def _sb_attn_seq(q3, kb3, vb3, sb2, t_real, tq=256, tk=256, hs=8):
    bsz, tp, _ = q3.shape
    wl = hs * HEAD_DIM
    n_last = t_real - (tp // tq - 1) * tq
    assert 0 < n_last <= tq and n_last % (2 * SUBLANES) == 0
    grid_spec = pltpu.PrefetchScalarGridSpec(
        num_scalar_prefetch=1,
        grid=(bsz, N_HEADS // hs, tp // tq),
        in_specs=[
            pl.BlockSpec((1, tq, wl), lambda b, h, i, sbr: (b, i, h)),
            pl.BlockSpec((1, tp, wl), lambda b, h, i, sbr: (b, 0, h)),
            pl.BlockSpec((1, tp, wl), lambda b, h, i, sbr: (b, 0, h)),
            pl.BlockSpec((tk, tk), lambda b, h, i, sbr: (0, 0)),
        ],
        out_specs=pl.BlockSpec((1, tq, wl), lambda b, h, i, sbr: (b, i, h)),
        scratch_shapes=[pltpu.VMEM((hs, tq, HEAD_DIM), F32), pltpu.VMEM((hs, tq, LANES), F32)],
    )
    return pl.pallas_call(
        functools.partial(_sb_attn_seq_kernel, tq=tq, tk=tk, hs=hs, n_last=n_last),
        grid_spec=grid_spec,
        out_shape=jax.ShapeDtypeStruct(q3.shape, BF16),
        compiler_params=_cparams(("parallel", "parallel", "arbitrary")),
        name="sb_attn_seq",
    )(sb2, q3, kb3, vb3, _neg_suffix(tk))


PAGE_ROWS = PAGE_SIZE * N_HEADS
PAGES_PER_STEP = 4


def _sb_attn_paged_kernel(pt_ref, q_ref, bias_ref, *refs, n_steps, pp):
    k_refs, v_refs = refs[:pp], refs[pp:2 * pp]
    nst_ref, o_ref, acc_scr, run_scr = refs[2 * pp:]
    step = pl.program_id(1)

    @pl.when(step == 0)
    def _():
        acc_scr[...] = jnp.zeros_like(acc_scr)
        run_scr[...] = jnp.zeros_like(run_scr)

    n_tiles = PAGE_ROWS // LANES
    q = q_ref[0]
    bias = jnp.concatenate([bias_ref[...]] * n_tiles, axis=1)
    own = ((lax.broadcasted_iota(jnp.int32, (N_HEADS, PAGE_ROWS), 1) & (N_HEADS - 1))
           == lax.broadcasted_iota(jnp.int32, (N_HEADS, PAGE_ROWS), 0))
    nst = nst_ref[...]
    run = run_scr[...]
    acc = acc_scr[...]
    zs = []
    for i in range(pp):
        kb = k_refs[i][0].astype(BF16)
        zs.append(lax.dot_general(q, kb, NT_DIMS, preferred_element_type=F32) + bias)
    css = []
    for i in range(pp):
        sp = jnp.where(own, _softplus2(zs[i]), 0.0)
        sp_st = jnp.concatenate([sp[:, t * LANES:(t + 1) * LANES] for t in range(n_tiles)], axis=0)
        css.append(jnp.dot(sp_st.astype(BF16), nst, preferred_element_type=F32))
    for i in range(pp):
        logw = [None] * n_tiles
        for t in reversed(range(n_tiles)):
            rows = slice(t * N_HEADS, (t + 1) * N_HEADS)
            logw[t] = zs[i][:, t * LANES:(t + 1) * LANES] + css[i][rows, :LANES] + run
            run = run + css[i][rows, LANES:]
        att = jnp.where(own, jnp.exp2(jnp.concatenate(logw, axis=1)), 0.0)
        vb = v_refs[i][0].astype(BF16)
        acc = acc + jnp.dot(att.astype(BF16), vb, preferred_element_type=F32)
    acc_scr[...] = acc
    run_scr[...] = run

    @pl.when(step == n_steps - 1)
    def _():
        o_ref[0] = acc


def _sb_attn_paged(q, cache_k, cache_v, page_table, sb2):
    assert N_HEADS & (N_HEADS - 1) == 0 and LANES % N_HEADS == 0
    nb, n_pages = page_table.shape
    n_pool = cache_k.shape[0]
    pp = PAGES_PER_STEP
    n_steps = n_pages // pp
    assert n_steps * pp == n_pages
    ck = cache_k.reshape(n_pool, PAGE_ROWS, HEAD_DIM)
    cv = cache_v.reshape(n_pool, PAGE_ROWS, HEAD_DIM)
    q3 = q.reshape(nb, N_HEADS, HEAD_DIM)
    bias = jnp.broadcast_to(sb2.astype(F32)[:, None], (N_HEADS, LANES))

    def page_spec(i):
        return pl.BlockSpec((1, PAGE_ROWS, HEAD_DIM),
                            lambda b, s, pt: (pt[b, n_pages - 1 - (s * pp + i)], 0, 0))

    grid_spec = pltpu.PrefetchScalarGridSpec(
        num_scalar_prefetch=1,
        grid=(nb, n_steps),
        in_specs=[
            pl.BlockSpec((1, N_HEADS, HEAD_DIM), lambda b, s, pt: (b, 0, 0)),
            pl.BlockSpec((N_HEADS, LANES), lambda b, s, pt: (0, 0)),
            *[page_spec(i) for i in range(pp)],
            *[page_spec(i) for i in range(pp)],
            pl.BlockSpec((LANES, 2 * LANES), lambda b, s, pt: (0, 0)),
        ],
        out_specs=pl.BlockSpec((1, N_HEADS, HEAD_DIM), lambda b, s, pt: (b, 0, 0)),
        scratch_shapes=[pltpu.VMEM((N_HEADS, HEAD_DIM), F32), pltpu.VMEM((N_HEADS, LANES), F32)],
    )
    o = pl.pallas_call(
        functools.partial(_sb_attn_paged_kernel, n_steps=n_steps, pp=pp),
        grid_spec=grid_spec,
        out_shape=jax.ShapeDtypeStruct((nb, N_HEADS, HEAD_DIM), F32),
        compiler_params=_cparams(("parallel", "arbitrary")),
        name="sb_attn_paged",
    )(page_table, q3, bias, *([ck] * pp), *([cv] * pp), _neg_suffix_and_total(LANES))
    return o.reshape(nb, D_MODEL)


T_PAD = 4352
TM_P = 1088
TM_DOWN = 1088
TN_NORM = 512
TN_MM = 512
SAMPLE_ROWS = 16


def _prompt_trunk(x, wts, wb):
    bsz, t_real, _ = x.shape
    pad = T_PAD - t_real
    x = jnp.pad(x, ((0, 0), (0, pad), (0, 0))).reshape(bsz * T_PAD, D_MODEL)
    tiles_per_seq = T_PAD // TM_P
    tail_row = (t_real - 1) % TM_P - (SUBLANES - 1)
    assert (t_real - 1) // TM_P == tiles_per_seq - 1 and tail_row % SUBLANES == 0
    conv_states, ffn_states = [], []
    k = v = kb = vb = None
    for l in range(DEPTH):
        if l < N_A_LAYERS:
            u = _glu(x, wts["a_g_norm"][l], (wb["pw1_a"][l], None), (wb["pw1_g"][l], None), 0,
                     TM_P, TN_NORM)
            u3 = u.reshape(bsz, T_PAD, D_MODEL)
            conv_states.append(u3[:, t_real - (CONV_A_WIDTH - 1):t_real])
            c = _dwconv_seq(u3, wts["a_w_dw"][l], wts["a_b_dw"][l])
            x = _norm_silu_mm_res(c.reshape(bsz * T_PAD, D_MODEL), wts["a_g_cn"][l],
                                  (wb["pw2"][l], None), x, TM_P, TN_NORM)
        else:
            j = l - N_A_LAYERS
            q = _q_proj(x, wts["b_g_norm"][j], (wb["q"][j], None), wts["b_g_q"][j], TM_P, TN_NORM)
            o = _sb_attn_seq(q.reshape(bsz, T_PAD, D_MODEL), kb, vb, wts["b_sb"][j] * LOG2E, t_real)
            x = _mm_res(o.reshape(bsz * T_PAD, D_MODEL), (wb["o"][j], None), x, TM_P, TN_MM,
                        "o_proj")
        act, tail = _ffn_gate_seq(x, wts["f_g_norm"][l], wb["gate"][l], wb["up"][l],
                                  wts["f_w_conv"][l], wts["f_b_conv"][l], TM_P, TN_NORM,
                                  tiles_per_seq, tail_row)
        tail = tail.reshape(bsz, tiles_per_seq, SUBLANES, D_FF)
        ffn_states.append(tail[:, tiles_per_seq - 1, SUBLANES - (FFN_CONV_WIDTH - 1):])
        x = _mm_res(act, (wb["down"][l], None), x, TM_DOWN, TN_MM, "ffn_down")
        if l == N_A_LAYERS - 1:
            k, v, kb, vb = _kv_proj(x, wts["kv_g_norm"], (wb["kv_k"], None), (wb["kv_v"], None), 0,
                                    wts["g_k"], TM_P, TN_NORM, tiles_per_seq, t_real)
            kb = kb.reshape(bsz, T_PAD, D_MODEL)
            vb = vb.reshape(bsz, T_PAD, D_MODEL)
    x = x.reshape(bsz, T_PAD, D_MODEL)[:, N_META:t_real]
    k = k.reshape(bsz, t_real, N_HEADS, HEAD_DIM)
    v = v.reshape(bsz, t_real, N_HEADS, HEAD_DIM)
    return x, jnp.stack(conv_states), jnp.stack(ffn_states), k, v


def _sample_trunk(x, state_conv_a, state_ffn, cache_k, cache_v, page_table, wts):
    nb = x.shape[0]
    m = SAMPLE_ROWS

    def pad_rows(a):
        return jnp.pad(a, ((0, m - nb), (0, 0)))

    x = pad_rows(x.reshape(nb, D_MODEL))
    conv_states, ffn_states = [], []
    k = v = None
    wb = {name: [] for name in ("pw1_a", "pw1_g", "pw2", "gate", "up", "down", "q", "o")}
    half = D_MODEL // TN_MM
    for l in range(DEPTH):
        if l < N_A_LAYERS:
            w1 = (wts["a_w_pw1"], l)
            u, wa, wg = _glu(x, wts["a_g_norm"][l], w1, w1, half, m, TN_MM)
            wb["pw1_a"].append(wa)
            wb["pw1_g"].append(wg)
            u = u[:nb]
            past = state_conv_a[l]
            conv_states.append(jnp.concatenate([past[:, 1:], u[:, None]], axis=1))
            c = _dwconv_step(past, u, wts["a_w_dw"][l], wts["a_b_dw"][l])
            x, w2 = _norm_silu_mm_res(pad_rows(c), wts["a_g_cn"][l], (wts["a_w_pw2"], l), x, m, TN_MM)
            wb["pw2"].append(w2)
        else:
            j = l - N_A_LAYERS
            q, wq = _q_proj(x, wts["b_g_norm"][j], (wts["b_w_q"], j), wts["b_g_q"][j], m, TN_MM)
            wb["q"].append(wq)
            o = _sb_attn_paged(q[:nb], cache_k, cache_v, page_table, wts["b_sb"][j] * LOG2E)
            x, wo = _mm_res(pad_rows(o).astype(BF16), (wts["b_w_o"], j), x, m, TN_MM, "o_proj")
            wb["o"].append(wo)
        past = state_ffn[l]
        act, gate, wgate, wup = _ffn_gate_step(
            x, wts["f_g_norm"][l], (wts["f_w_gate"], l), (wts["f_w_up"], l), wts["f_w_conv"][l],
            wts["f_b_conv"][l], pad_rows(past[:, 0]), pad_rows(past[:, 1]), TN_MM)
        wb["gate"].append(wgate)
        wb["up"].append(wup)
        ffn_states.append(jnp.stack([past[:, 1], gate[:nb]], axis=1))
        x, wdown = _mm_res(act, (wts["f_w_down"], l), x, m, TN_MM, "ffn_down")
        wb["down"].append(wdown)
        if l == N_A_LAYERS - 1:
            wkv = (wts["w_kv"], None)
            k, v, _, _, wb["kv_k"], wb["kv_v"] = _kv_proj(x, wts["kv_g_norm"], wkv, wkv, half,
                                                          wts["g_k"], m, TN_MM, 1, nb)
    y = x[:nb].reshape(nb, 1, D_MODEL)
    k = k.reshape(nb, 1, N_HEADS, HEAD_DIM)
    v = v.reshape(nb, 1, N_HEADS, HEAD_DIM)
    return (y, jnp.stack(conv_states), jnp.stack(ffn_states), k, v), wb


def kernel(x_prompt, x_sample, state_conv_a, state_ffn_conv, cache_k, cache_v, page_table, meta_tokens, a_g_norm, a_w_pw1, a_w_dw, a_b_dw, a_g_cn, a_w_pw2, f_g_norm, f_w_gate, f_w_up, f_w_down, f_w_conv, f_b_conv, kv_g_norm, w_kv, g_k, b_g_norm, b_w_q, b_g_q, b_sb, b_w_o):
    wts = dict(a_g_norm=a_g_norm, a_w_pw1=a_w_pw1, a_w_dw=a_w_dw, a_b_dw=a_b_dw, a_g_cn=a_g_cn,
               a_w_pw2=a_w_pw2, f_g_norm=f_g_norm, f_w_gate=f_w_gate, f_w_up=f_w_up,
               f_w_down=f_w_down, f_w_conv=f_w_conv, f_b_conv=f_b_conv, kv_g_norm=kv_g_norm,
               w_kv=w_kv, g_k=g_k, b_g_norm=b_g_norm, b_w_q=b_w_q, b_g_q=b_g_q, b_sb=b_sb,
               b_w_o=b_w_o)
    (y_s, conv_s, ffn_s, k_s, v_s), wb = _sample_trunk(x_sample, state_conv_a, state_ffn_conv,
                                                       cache_k, cache_v, page_table, wts)
    bp = x_prompt.shape[0]
    meta = jnp.broadcast_to(meta_tokens[None], (bp, N_META, D_MODEL))
    xp = jnp.concatenate([meta, x_prompt], axis=1)
    y_p, conv_p, ffn_p, k_p, v_p = _prompt_trunk(xp, wts, wb)
    return (y_p, y_s, conv_p, ffn_p, k_p, v_p, conv_s, ffn_s, k_s, v_s)
```

```python
import functools

import jax
import jax.numpy as jnp
from jax import lax
from jax.experimental import pallas as pl
from jax.experimental.pallas import tpu as pltpu

D_MODEL = 2048
D_FF = 5632
N_HEADS = 16
HEAD_DIM = 128
N_META = 16
N_A_LAYERS = 2
DEPTH = 4
CONV_A_WIDTH = 31
FFN_CONV_WIDTH = 3
PAGE_SIZE = 128
EPS = 1e-6

LANES = 128
SUBLANES = 8
VMEM_LIMIT = 56 * 1024 * 1024

F32 = jnp.float32
BF16 = jnp.bfloat16


def _cparams(sem):
    return pltpu.CompilerParams(dimension_semantics=sem, vmem_limit_bytes=VMEM_LIMIT)


def _rms_rows(x, g):
    ms = jnp.mean(x * x, axis=-1, keepdims=True)
    return x * lax.rsqrt(ms + EPS) * g


def _head_rms(t, gh):
    parts = []
    for s in range(t.shape[1] // HEAD_DIM):
        parts.append(_rms_rows(t[:, s * HEAD_DIM:(s + 1) * HEAD_DIM], gh))
    return parts


LOG2E = 1.4426950408889634

Q_SCALE = HEAD_DIM ** -0.5 * LOG2E


def _softplus2(z2):
    return jnp.maximum(z2, 0.0) + jnp.log(1.0 + jnp.exp2(-jnp.abs(z2))) * LOG2E


ROW_CHUNK = 272


def _row_chunks(tm):
    rc = ROW_CHUNK if tm % ROW_CHUNK == 0 else tm
    return [slice(r0, r0 + rc) for r0 in range(0, tm, rc)]


def _bf16_weights(w_ref, wb_ref):
    if wb_ref is None:
        return w_ref
    wb_ref[...] = w_ref[...].astype(BF16)
    return wb_ref


def _weight_outs(emit, n_w, k, n, tn):
    if not emit:
        return [], []
    spec = pl.BlockSpec((k, tn), lambda *ij: (0, ij[-1]))
    return [spec] * n_w, [jax.ShapeDtypeStruct((k, n), BF16)] * n_w


def _staged_chunks(h_scr, stage, consume):
    chunks = _row_chunks(h_scr.shape[0])

    @pl.when(pl.program_id(1) == 0)
    def _():
        for rows in chunks:
            h_scr[rows, :] = stage(rows)
            consume(rows)

    @pl.when(pl.program_id(1) != 0)
    def _():
        for rows in chunks:
            consume(rows)


def _glu_kernel(x_ref, g_ref, wa_ref, wg_ref, u_ref, *rest, emit):
    wa_o, wg_o, h_scr = rest if emit else (None, None, *rest)
    wa_ref, wg_ref = _bf16_weights(wa_ref, wa_o), _bf16_weights(wg_ref, wg_o)

    def consume(rows):
        h = h_scr[rows, :]
        a = jnp.dot(h, wa_ref[...], preferred_element_type=F32)
        gt = jnp.dot(h, wg_ref[...], preferred_element_type=F32)
        u_ref[rows, :] = a * jax.nn.sigmoid(gt)

    _staged_chunks(h_scr, lambda rows: _rms_rows(x_ref[rows, :], g_ref[...]).astype(BF16), consume)


def _w_spec(w, tn, col0=0):
    arr, layer = w
    k = arr.shape[-2]
    if layer is None:
        return pl.BlockSpec((k, tn), lambda *ij: (0, ij[-1] + col0))
    return pl.BlockSpec((None, k, tn), lambda *ij: (layer, 0, ij[-1] + col0))


def _glu(x, g, w_a, w_g, g_col0, tm, tn):
    m = x.shape[0]
    emit = w_a[0].dtype != BF16
    w_specs, w_shapes = _weight_outs(emit, 2, D_MODEL, D_MODEL, tn)
    out = pl.pallas_call(
        functools.partial(_glu_kernel, emit=emit),
        grid=(m // tm, D_MODEL // tn),
        in_specs=[
            pl.BlockSpec((tm, D_MODEL), lambda i, j: (i, 0)),
            pl.BlockSpec((1, D_MODEL), lambda i, j: (0, 0)),
            _w_spec(w_a, tn),
            _w_spec(w_g, tn, g_col0),
        ],
        out_specs=[pl.BlockSpec((tm, tn), lambda i, j: (i, j))] + w_specs,
        out_shape=[jax.ShapeDtypeStruct((m, D_MODEL), F32)] + w_shapes,
        scratch_shapes=[pltpu.VMEM((tm, D_MODEL), BF16)],
        compiler_params=_cparams(("parallel", "arbitrary")),
        name="glu",
    )(x, g.reshape(1, D_MODEL), w_a[0], w_g[0])
    return out if emit else out[0]


DW_HALO = 32


def _dwconv_seq_kernel(u_ref, halo_ref, w_ref, b_ref, o_ref, full_scr, sh_scr, *, tt, cl):
    first = pl.program_id(1) == 0
    full_scr[0:DW_HALO, :] = jnp.where(first, 0.0, halo_ref[0])
    full_scr[DW_HALO:DW_HALO + tt, :] = u_ref[0]
    lead = DW_HALO - (CONV_A_WIDTH - 1)
    sh_rows = sh_scr.shape[1]
    for c in range(cl // LANES):
        ls = slice(c * LANES, (c + 1) * LANES)
        for s in range(1, SUBLANES):
            sh_scr[s - 1, :, ls] = full_scr[s:s + sh_rows, ls]
        acc = jnp.broadcast_to(b_ref[:, ls], (tt, LANES))
        for k in range(CONV_A_WIDTH):
            s, a0 = (lead + k) % SUBLANES, (lead + k) // SUBLANES * SUBLANES
            rows = full_scr[a0:a0 + tt, ls] if s == 0 else sh_scr[s - 1, a0:a0 + tt, ls]
            acc = acc + w_ref[k:k + 1, ls] * rows
        o_ref[0, :, ls] = acc


def _dwconv_seq(u3, w, b, tt=256, cl=1024):
    bsz, tp, _ = u3.shape
    w_pad = jnp.zeros((DW_HALO, D_MODEL), F32).at[:CONV_A_WIDTH].set(w)
    hb = tt // DW_HALO
    return pl.pallas_call(
        functools.partial(_dwconv_seq_kernel, tt=tt, cl=cl),
        grid=(bsz, tp // tt, D_MODEL // cl),
        in_specs=[
            pl.BlockSpec((1, tt, cl), lambda bb, i, c: (bb, i, c)),
            pl.BlockSpec((1, DW_HALO, cl), lambda bb, i, c: (bb, jnp.maximum(i * hb - 1, 0), c)),
            pl.BlockSpec((DW_HALO, cl), lambda bb, i, c: (0, c)),
            pl.BlockSpec((1, cl), lambda bb, i, c: (0, c)),
        ],
        out_specs=pl.BlockSpec((1, tt, cl), lambda bb, i, c: (bb, i, c)),
        out_shape=jax.ShapeDtypeStruct(u3.shape, F32),
        scratch_shapes=[pltpu.VMEM((DW_HALO + tt, cl), F32),
                        pltpu.VMEM((SUBLANES - 1, DW_HALO + tt - SUBLANES, cl), F32)],
        compiler_params=_cparams(("parallel", "parallel", "parallel")),
        name="dwconv_seq",
    )(u3, u3, w_pad, b.reshape(1, D_MODEL))


def _dwconv_step_kernel(s_ref, u_ref, w_ref, b_ref, o_ref):
    w = w_ref[...]
    past = jnp.sum(s_ref[...] * w[None, :CONV_A_WIDTH - 1, :], axis=1)
    o_ref[...] = past + u_ref[...] * w[CONV_A_WIDTH - 1:CONV_A_WIDTH, :] + b_ref[...]


def _dwconv_step(state, u, w, b):
    nb = state.shape[0]
    return pl.pallas_call(
        _dwconv_step_kernel,
        out_shape=jax.ShapeDtypeStruct((nb, D_MODEL), F32),
        compiler_params=pltpu.CompilerParams(vmem_limit_bytes=VMEM_LIMIT),
        name="dwconv_step",
    )(state, u, w, b.reshape(1, D_MODEL))


def _norm_silu_mm_res_kernel(c_ref, g_ref, w_ref, x_ref, o_ref, *rest, emit):
    w_o, h_scr = rest if emit else (None, *rest)
    w_ref = _bf16_weights(w_ref, w_o)

    def stage(rows):
        y = _rms_rows(c_ref[rows, :], g_ref[...])
        return (y * jax.nn.sigmoid(y)).astype(BF16)

    def consume(rows):
        o_ref[rows, :] = x_ref[rows, :] + jnp.dot(h_scr[rows, :], w_ref[...],
                                                  preferred_element_type=F32)

    _staged_chunks(h_scr, stage, consume)


def _norm_silu_mm_res(c, g, w, xres, tm, tn):
    m = c.shape[0]
    emit = w[0].dtype != BF16
    w_specs, w_shapes = _weight_outs(emit, 1, D_MODEL, D_MODEL, tn)
    out = pl.pallas_call(
        functools.partial(_norm_silu_mm_res_kernel, emit=emit),
        grid=(m // tm, D_MODEL // tn),
        in_specs=[
            pl.BlockSpec((tm, D_MODEL), lambda i, j: (i, 0)),
            pl.BlockSpec((1, D_MODEL), lambda i, j: (0, 0)),
            _w_spec(w, tn),
            pl.BlockSpec((tm, tn), lambda i, j: (i, j)),
        ],
        out_specs=[pl.BlockSpec((tm, tn), lambda i, j: (i, j))] + w_specs,
        out_shape=[jax.ShapeDtypeStruct((m, D_MODEL), F32)] + w_shapes,
        scratch_shapes=[pltpu.VMEM((tm, D_MODEL), BF16)],
        compiler_params=_cparams(("parallel", "arbitrary")),
        name="pw2",
    )(c, g.reshape(1, D_MODEL), w[0], xres)
    return out if emit else out[0]


def _ffn_gate_seq_kernel(x_ref, xp_ref, g_ref, wg_ref, wu_ref, wc_ref, bc_ref,
                         act_ref, tail_ref, h_scr, hp_scr, gs_scr, *, tm, tiles_per_seq, tail_row):
    i = pl.program_id(0)

    @pl.when(pl.program_id(1) == 0)
    def _():
        h_scr[...] = _rms_rows(x_ref[...], g_ref[...]).astype(BF16)
        hp = _rms_rows(xp_ref[...], g_ref[...])
        hp = jnp.where(i % tiles_per_seq == 0, 0.0, hp)
        hp_scr[...] = jnp.concatenate([hp, hp], axis=0).astype(BF16)

    gprev = jnp.dot(hp_scr[...], wg_ref[...], preferred_element_type=F32)
    gs_scr[0:SUBLANES, :] = gprev[0:SUBLANES]

    def consume(rows):
        r0, r1 = rows.start, rows.stop
        h = h_scr[rows, :]
        gate = jnp.dot(h, wg_ref[...], preferred_element_type=F32)
        up = jnp.dot(h, wu_ref[...], preferred_element_type=F32)
        gs_scr[SUBLANES + r0:SUBLANES + r1, :] = gate
        g1 = gs_scr[SUBLANES - 1 + r0:SUBLANES - 1 + r1, :]
        g2 = gs_scr[SUBLANES - 2 + r0:SUBLANES - 2 + r1, :]
        gc = wc_ref[0:1, :] * g2 + wc_ref[1:2, :] * g1 + wc_ref[2:3, :] * gate + bc_ref[...]
        act_ref[rows, :] = (gc * jax.nn.sigmoid(gc) * up).astype(BF16)
        if r0 <= tail_row < r1:
            tail_ref[0] = gate[tail_row - r0:tail_row - r0 + SUBLANES]

    for rows in _row_chunks(tm):
        consume(rows)


def _ffn_gate_seq(x, g, wg_bf, wu_bf, wc, bc, tm, tn, tiles_per_seq, tail_row):
    m = x.shape[0]
    nt = m // tm
    hb = tm // SUBLANES
    wc_pad = jnp.zeros((SUBLANES, D_FF), F32).at[:FFN_CONV_WIDTH].set(wc)
    return pl.pallas_call(
        functools.partial(_ffn_gate_seq_kernel, tm=tm, tiles_per_seq=tiles_per_seq, tail_row=tail_row),
        grid=(nt, D_FF // tn),
        in_specs=[
            pl.BlockSpec((tm, D_MODEL), lambda i, j: (i, 0)),
            pl.BlockSpec((SUBLANES, D_MODEL), lambda i, j: (jnp.maximum(i * hb - 1, 0), 0)),
            pl.BlockSpec((1, D_MODEL), lambda i, j: (0, 0)),
            pl.BlockSpec((D_MODEL, tn), lambda i, j: (0, j)),
            pl.BlockSpec((D_MODEL, tn), lambda i, j: (0, j)),
            pl.BlockSpec((SUBLANES, tn), lambda i, j: (0, j)),
            pl.BlockSpec((1, tn), lambda i, j: (0, j)),
        ],
        out_specs=[
            pl.BlockSpec((tm, tn), lambda i, j: (i, j)),
            pl.BlockSpec((1, SUBLANES, tn), lambda i, j: (i, 0, j)),
        ],
        out_shape=[
            jax.ShapeDtypeStruct((m, D_FF), BF16),
            jax.ShapeDtypeStruct((nt, SUBLANES, D_FF), F32),
        ],
        scratch_shapes=[
            pltpu.VMEM((tm, D_MODEL), BF16),
            pltpu.VMEM((2 * SUBLANES, D_MODEL), BF16),
            pltpu.VMEM((SUBLANES + tm, tn), F32),
        ],
        compiler_params=_cparams(("parallel", "arbitrary")),
        name="ffn_gate_seq",
    )(x, x, g.reshape(1, D_MODEL), wg_bf, wu_bf, wc_pad, bc.reshape(1, D_FF))


def _ffn_gate_step_kernel(x_ref, g_ref, wg_ref, wu_ref, wc_ref, bc_ref, s0_ref, s1_ref,
                          act_ref, gate_ref, wg_o, wu_o, h_scr):
    @pl.when(pl.program_id(0) == 0)
    def _():
        h_scr[...] = _rms_rows(x_ref[...], g_ref[...]).astype(BF16)

    h = h_scr[...]
    gate = jnp.dot(h, _bf16_weights(wg_ref, wg_o)[...], preferred_element_type=F32)
    up = jnp.dot(h, _bf16_weights(wu_ref, wu_o)[...], preferred_element_type=F32)
    gc = (wc_ref[0:1, :] * s0_ref[...] + wc_ref[1:2, :] * s1_ref[...]
          + wc_ref[2:3, :] * gate + bc_ref[...])
    act_ref[...] = (gc * jax.nn.sigmoid(gc) * up).astype(BF16)
    gate_ref[...] = gate


def _ffn_gate_step(x, g, wg, wu, wc, bc, s0, s1, tn):
    m = x.shape[0]
    wc_pad = jnp.zeros((SUBLANES, D_FF), F32).at[:FFN_CONV_WIDTH].set(wc)
    w_specs, w_shapes = _weight_outs(True, 2, D_MODEL, D_FF, tn)
    return pl.pallas_call(
        _ffn_gate_step_kernel,
        grid=(D_FF // tn,),
        in_specs=[
            pl.BlockSpec((m, D_MODEL), lambda j: (0, 0)),
            pl.BlockSpec((1, D_MODEL), lambda j: (0, 0)),
            _w_spec(wg, tn),
            _w_spec(wu, tn),
            pl.BlockSpec((SUBLANES, tn), lambda j: (0, j)),
            pl.BlockSpec((1, tn), lambda j: (0, j)),
            pl.BlockSpec((m, tn), lambda j: (0, j)),
            pl.BlockSpec((m, tn), lambda j: (0, j)),
        ],
        out_specs=[
            pl.BlockSpec((m, tn), lambda j: (0, j)),
            pl.BlockSpec((m, tn), lambda j: (0, j)),
        ] + w_specs,
        out_shape=[
            jax.ShapeDtypeStruct((m, D_FF), BF16),
            jax.ShapeDtypeStruct((m, D_FF), F32),
        ] + w_shapes,
        scratch_shapes=[pltpu.VMEM((m, D_MODEL), BF16)],
        compiler_params=_cparams(("arbitrary",)),
        name="ffn_gate_step",
    )(x, g.reshape(1, D_MODEL), wg[0], wu[0], wc_pad, bc.reshape(1, D_FF), s0, s1)


def _mm_res_kernel(a_ref, w_ref, x_ref, o_ref, w_o=None):
    w_ref = _bf16_weights(w_ref, w_o)
    for rows in _row_chunks(o_ref.shape[0]):
        o_ref[rows, :] = x_ref[rows, :] + jnp.dot(a_ref[rows, :], w_ref[...],
                                                  preferred_element_type=F32)


def _mm_res(a_bf, w, xres, tm, tn, name):
    m, kdim = a_bf.shape
    n = w[0].shape[-1]
    emit = w[0].dtype != BF16
    w_specs, w_shapes = _weight_outs(emit, 1, kdim, n, tn)
    out = pl.pallas_call(
        _mm_res_kernel,
        grid=(m // tm, n // tn),
        in_specs=[
            pl.BlockSpec((tm, kdim), lambda i, j: (i, 0)),
            _w_spec(w, tn),
            pl.BlockSpec((tm, tn), lambda i, j: (i, j)),
        ],
        out_specs=[pl.BlockSpec((tm, tn), lambda i, j: (i, j))] + w_specs,
        out_shape=[jax.ShapeDtypeStruct((m, n), F32)] + w_shapes,
        compiler_params=_cparams(("parallel", "parallel")),
        name=name,
    )(a_bf, w[0], xres)
    return out if emit else out[0]


def _kv_kernel(x_ref, g_ref, wk_ref, wv_ref, gk_ref, k_ref, v_ref, kb_ref, vb_ref, *rest, emit):
    wk_o, wv_o, h_scr = rest if emit else (None, None, *rest)
    wk_ref, wv_ref = _bf16_weights(wk_ref, wk_o), _bf16_weights(wv_ref, wv_o)

    def consume(rows):
        h = h_scr[rows, :]
        kraw = jnp.dot(h, wk_ref[...], preferred_element_type=F32)
        v = jnp.dot(h, wv_ref[...], preferred_element_type=F32)
        for s, kn in enumerate(_head_rms(kraw, gk_ref[...])):
            hc = slice(s * HEAD_DIM, (s + 1) * HEAD_DIM)
            k_ref[rows, hc] = kn
            kb_ref[rows, hc] = kn.astype(BF16)
        v_ref[rows, :] = v
        vb_ref[rows, :] = v.astype(BF16)

    _staged_chunks(h_scr, lambda rows: _rms_rows(x_ref[rows, :], g_ref[...]).astype(BF16), consume)


def _kv_proj(x, g, w_k, w_v, v_col0, gk, tm, tn, tiles_per_seq, t_out):
    m = x.shape[0]
    bsz = m // (tm * tiles_per_seq)
    emit = w_k[0].dtype != BF16
    w_specs, w_shapes = _weight_outs(emit, 2, D_MODEL, D_MODEL, tn)
    blk = pl.BlockSpec((tm, tn), lambda i, j: (i, j))
    blk_seq = pl.BlockSpec((None, tm, tn), lambda i, j: (i // tiles_per_seq, i % tiles_per_seq, j))
    return pl.pallas_call(
        functools.partial(_kv_kernel, emit=emit),
        grid=(m // tm, D_MODEL // tn),
        in_specs=[
            pl.BlockSpec((tm, D_MODEL), lambda i, j: (i, 0)),
            pl.BlockSpec((1, D_MODEL), lambda i, j: (0, 0)),
            _w_spec(w_k, tn),
            _w_spec(w_v, tn, v_col0),
            pl.BlockSpec((1, HEAD_DIM), lambda i, j: (0, 0)),
        ],
        out_specs=[blk_seq, blk_seq, blk, blk] + w_specs,
        out_shape=[
            jax.ShapeDtypeStruct((bsz, t_out, D_MODEL), F32),
            jax.ShapeDtypeStruct((bsz, t_out, D_MODEL), F32),
            jax.ShapeDtypeStruct((m, D_MODEL), BF16),
            jax.ShapeDtypeStruct((m, D_MODEL), BF16),
        ] + w_shapes,
        scratch_shapes=[pltpu.VMEM((tm, D_MODEL), BF16)],
        compiler_params=_cparams(("parallel", "arbitrary")),
        name="kv_proj",
    )(x, g.reshape(1, D_MODEL), w_k[0], w_v[0], gk.reshape(1, HEAD_DIM))


def _q_kernel(x_ref, g_ref, w_ref, gq_ref, q_ref, *rest, emit):
    w_o, h_scr = rest if emit else (None, *rest)
    w_ref = _bf16_weights(w_ref, w_o)

    def consume(rows):
        qraw = jnp.dot(h_scr[rows, :], w_ref[...], preferred_element_type=F32)
        for s, qn in enumerate(_head_rms(qraw, gq_ref[...])):
            q_ref[rows, s * HEAD_DIM:(s + 1) * HEAD_DIM] = (qn * Q_SCALE).astype(BF16)

    _staged_chunks(h_scr, lambda rows: _rms_rows(x_ref[rows, :], g_ref[...]).astype(BF16), consume)


def _q_proj(x, g, w, gq, tm, tn):
    m = x.shape[0]
    emit = w[0].dtype != BF16
    w_specs, w_shapes = _weight_outs(emit, 1, D_MODEL, D_MODEL, tn)
    out = pl.pallas_call(
        functools.partial(_q_kernel, emit=emit),
        grid=(m // tm, D_MODEL // tn),
        in_specs=[
            pl.BlockSpec((tm, D_MODEL), lambda i, j: (i, 0)),
            pl.BlockSpec((1, D_MODEL), lambda i, j: (0, 0)),
            _w_spec(w, tn),
            pl.BlockSpec((1, HEAD_DIM), lambda i, j: (0, 0)),
        ],
        out_specs=[pl.BlockSpec((tm, tn), lambda i, j: (i, j))] + w_specs,
        out_shape=[jax.ShapeDtypeStruct((m, D_MODEL), BF16)] + w_shapes,
        scratch_shapes=[pltpu.VMEM((tm, D_MODEL), BF16)],
        compiler_params=_cparams(("parallel", "arbitrary")),
        name="q_proj",
    )(x, g.reshape(1, D_MODEL), w[0], gq.reshape(1, HEAD_DIM))
    return out if emit else out[0]


NT_DIMS = (((1,), (1,)), ((), ()))


def _neg_suffix(tk):
    r = lax.broadcasted_iota(jnp.int32, (tk, tk), 0)
    c = lax.broadcasted_iota(jnp.int32, (tk, tk), 1)
    return -(r >= c).astype(BF16)


def _neg_suffix_and_total(tk):
    return jnp.concatenate([_neg_suffix(tk), -jnp.ones((tk, tk), BF16)], axis=1)


def _sb_attn_seq_kernel(sb_ref, q_ref, k_ref, v_ref, nst_ref, o_ref, acc_scr, run_scr,
                        *, tq, tk, hs, n_last):
    hg = pl.program_id(1)
    qi = pl.program_id(2)
    sub = tq // tk
    nst = nst_ref[...]
    lanes = [slice(s * HEAD_DIM, (s + 1) * HEAD_DIM) for s in range(hs)]

    def logits(j, r0, nr):
        off = pl.multiple_of(j * tk, tk)
        return [lax.dot_general(q_ref[0, r0:nr, lanes[s]], k_ref[0, pl.ds(off, tk), lanes[s]],
                                NT_DIMS, preferred_element_type=F32) + sb_ref[hg * hs + s]
                for s in range(hs)]

    def block(j, r0, nr, masked):
        rows = nr - r0
        off = pl.multiple_of(j * tk, tk)
        zs = logits(j, r0, nr)
        if masked:
            diff = (lax.broadcasted_iota(jnp.int32, (rows, tk), 0)
                    - lax.broadcasted_iota(jnp.int32, (rows, tk), 1))
            mask = diff > (j * tk - qi * tq - r0)
        css = []
        for s in range(hs):
            sp = _softplus2(zs[s])
            if masked:
                sp = jnp.where(mask, sp, 0.0)
            css.append(jnp.dot(sp.astype(BF16), nst, preferred_element_type=F32))
        for s in range(hs):
            run = run_scr[s, r0:nr, :]
            att = jnp.exp2(zs[s] + css[s] + jnp.concatenate([run] * (tk // LANES), axis=1))
            if masked:
                att = jnp.where(mask, att, 0.0)
            vs = v_ref[0, pl.ds(off, tk), lanes[s]]
            acc_scr[s, r0:nr, :] += jnp.dot(att.astype(BF16), vs, preferred_element_type=F32)
            run_scr[s, r0:nr, :] = run + jnp.broadcast_to(css[s][:, 0:1], (rows, LANES))

    def tile(nr):
        acc_scr[:, 0:nr, :] = jnp.zeros((hs, nr, HEAD_DIM), F32)
        run_scr[:, 0:nr, :] = jnp.zeros((hs, nr, LANES), F32)
        for d in reversed(range(sub)):
            if d * tk < nr:
                block(qi * sub + d, d * tk, nr, True)

        def body(jj, carry):
            block(qi * sub - 1 - jj, 0, nr, False)
            return carry

        lax.fori_loop(0, qi * sub, body, 0)
        for s in range(hs):
            o_ref[0, 0:nr, lanes[s]] = acc_scr[s, 0:nr, :].astype(BF16)
        if nr < tq:
            o_ref[0, nr:tq, :] = jnp.zeros((tq - nr, hs * HEAD_DIM), BF16)

    if n_last == tq:
        tile(tq)
    else:
        last = pl.num_programs(2) - 1
        pl.when(qi != last)(lambda: tile(tq))
        pl.when(qi == last)(lambda: tile(n_last))


def _sb_attn_seq(q3, kb3, vb3, sb2, t_real, tq=256, tk=256, hs=8):
    bsz, tp, _ = q3.shape
    wl = hs * HEAD_DIM
    n_last = t_real - (tp // tq - 1) * tq
    assert 0 < n_last <= tq and n_last % (2 * SUBLANES) == 0
    grid_spec = pltpu.PrefetchScalarGridSpec(
        num_scalar_prefetch=1,
        grid=(bsz, N_HEADS // hs, tp // tq),
        in_specs=[
            pl.BlockSpec((1, tq, wl), lambda b, h, i, sbr: (b, i, h)),
            pl.BlockSpec((1, tp, wl), lambda b, h, i, sbr: (b, 0, h)),
            pl.BlockSpec((1, tp, wl), lambda b, h, i, sbr: (b, 0, h)),
            pl.BlockSpec((tk, tk), lambda b, h, i, sbr: (0, 0)),
        ],
        out_specs=pl.BlockSpec((1, tq, wl), lambda b, h, i, sbr: (b, i, h)),
        scratch_shapes=[pltpu.VMEM((hs, tq, HEAD_DIM), F32), pltpu.VMEM((hs, tq, LANES), F32)],
    )
    return pl.pallas_call(
        functools.partial(_sb_attn_seq_kernel, tq=tq, tk=tk, hs=hs, n_last=n_last),
        grid_spec=grid_spec,
        out_shape=jax.ShapeDtypeStruct(q3.shape, BF16),
        compiler_params=_cparams(("parallel", "parallel", "arbitrary")),
        name="sb_attn_seq",
    )(sb2, q3, kb3, vb3, _neg_suffix(tk))


PAGE_ROWS = PAGE_SIZE * N_HEADS
PAGES_PER_STEP = 8


def _sb_attn_paged_kernel(pt_ref, q_ref, bias_ref, *refs, n_steps, pp):
    k_refs, v_refs = refs[:pp], refs[pp:2 * pp]
    nst_ref, o_ref, acc_scr, run_scr = refs[2 * pp:]
    step = pl.program_id(1)

    @pl.when(step == 0)
    def _():
        acc_scr[...] = jnp.zeros_like(acc_scr)
        run_scr[...] = jnp.zeros_like(run_scr)

    n_tiles = PAGE_ROWS // LANES
    q = q_ref[0]
    bias = jnp.concatenate([bias_ref[...]] * n_tiles, axis=1)
    own = ((lax.broadcasted_iota(jnp.int32, (N_HEADS, PAGE_ROWS), 1) & (N_HEADS - 1))
           == lax.broadcasted_iota(jnp.int32, (N_HEADS, PAGE_ROWS), 0))
    nst = nst_ref[...]
    run = run_scr[...]
    acc = acc_scr[...]
    zs = []
    for i in range(pp):
        kb = k_refs[i][0].astype(BF16)
        zs.append(lax.dot_general(q, kb, NT_DIMS, preferred_element_type=F32) + bias)
    css = []
    for i in range(pp):
        sp = jnp.where(own, _softplus2(zs[i]), 0.0)
        sp_st = jnp.concatenate([sp[:, t * LANES:(t + 1) * LANES] for t in range(n_tiles)], axis=0)
        css.append(jnp.dot(sp_st.astype(BF16), nst, preferred_element_type=F32))
    for i in range(pp):
        logw = [None] * n_tiles
        for t in reversed(range(n_tiles)):
            rows = slice(t * N_HEADS, (t + 1) * N_HEADS)
            logw[t] = zs[i][:, t * LANES:(t + 1) * LANES] + css[i][rows, :LANES] + run
            run = run + css[i][rows, LANES:]
        att = jnp.where(own, jnp.exp2(jnp.concatenate(logw, axis=1)), 0.0)
        vb = v_refs[i][0].astype(BF16)
        acc = acc + jnp.dot(att.astype(BF16), vb, preferred_element_type=F32)
    acc_scr[...] = acc
    run_scr[...] = run

    @pl.when(step == n_steps - 1)
    def _():
        o_ref[0] = acc


def _sb_attn_paged(q, cache_k, cache_v, page_table, sb2):
    assert N_HEADS & (N_HEADS - 1) == 0 and LANES % N_HEADS == 0
    nb, n_pages = page_table.shape
    n_pool = cache_k.shape[0]
    pp = PAGES_PER_STEP
    n_steps = n_pages // pp
    assert n_steps * pp == n_pages
    ck = cache_k.reshape(n_pool, PAGE_ROWS, HEAD_DIM)
    cv = cache_v.reshape(n_pool, PAGE_ROWS, HEAD_DIM)
    q3 = q.reshape(nb, N_HEADS, HEAD_DIM)
    bias = jnp.broadcast_to(sb2.astype(F32)[:, None], (N_HEADS, LANES))

    def page_spec(i):
        return pl.BlockSpec((1, PAGE_ROWS, HEAD_DIM),
                            lambda b, s, pt: (pt[b, n_pages - 1 - (s * pp + i)], 0, 0))

    grid_spec = pltpu.PrefetchScalarGridSpec(
        num_scalar_prefetch=1,
        grid=(nb, n_steps),
        in_specs=[
            pl.BlockSpec((1, N_HEADS, HEAD_DIM), lambda b, s, pt: (b, 0, 0)),
            pl.BlockSpec((N_HEADS, LANES), lambda b, s, pt: (0, 0)),
            *[page_spec(i) for i in range(pp)],
            *[page_spec(i) for i in range(pp)],
            pl.BlockSpec((LANES, 2 * LANES), lambda b, s, pt: (0, 0)),
        ],
        out_specs=pl.BlockSpec((1, N_HEADS, HEAD_DIM), lambda b, s, pt: (b, 0, 0)),
        scratch_shapes=[pltpu.VMEM((N_HEADS, HEAD_DIM), F32), pltpu.VMEM((N_HEADS, LANES), F32)],
    )
    o = pl.pallas_call(
        functools.partial(_sb_attn_paged_kernel, n_steps=n_steps, pp=pp),
        grid_spec=grid_spec,
        out_shape=jax.ShapeDtypeStruct((nb, N_HEADS, HEAD_DIM), F32),
        compiler_params=_cparams(("parallel", "arbitrary")),
        name="sb_attn_paged",
    )(page_table, q3, bias, *([ck] * pp), *([cv] * pp), _neg_suffix_and_total(LANES))
    return o.reshape(nb, D_MODEL)


T_PAD = 4352
TM_P = 1088
TM_DOWN = 1088
TN_NORM = 512
TN_MM = 512
SAMPLE_ROWS = 16


def _prompt_trunk(x, wts, wb):
    bsz, t_real, _ = x.shape
    pad = T_PAD - t_real
    x = jnp.pad(x, ((0, 0), (0, pad), (0, 0))).reshape(bsz * T_PAD, D_MODEL)
    tiles_per_seq = T_PAD // TM_P
    tail_row = (t_real - 1) % TM_P - (SUBLANES - 1)
    assert (t_real - 1) // TM_P == tiles_per_seq - 1 and tail_row % SUBLANES == 0
    conv_states, ffn_states = [], []
    k = v = kb = vb = None
    for l in range(DEPTH):
        if l < N_A_LAYERS:
            u = _glu(x, wts["a_g_norm"][l], (wb["pw1_a"][l], None), (wb["pw1_g"][l], None), 0,
                     TM_P, TN_NORM)
            u3 = u.reshape(bsz, T_PAD, D_MODEL)
            conv_states.append(u3[:, t_real - (CONV_A_WIDTH - 1):t_real])
            c = _dwconv_seq(u3, wts["a_w_dw"][l], wts["a_b_dw"][l])
            x = _norm_silu_mm_res(c.reshape(bsz * T_PAD, D_MODEL), wts["a_g_cn"][l],
                                  (wb["pw2"][l], None), x, TM_P, TN_NORM)
        else:
            j = l - N_A_LAYERS
            q = _q_proj(x, wts["b_g_norm"][j], (wb["q"][j], None), wts["b_g_q"][j], TM_P, TN_NORM)
            o = _sb_attn_seq(q.reshape(bsz, T_PAD, D_MODEL), kb, vb, wts["b_sb"][j] * LOG2E, t_real)
            x = _mm_res(o.reshape(bsz * T_PAD, D_MODEL), (wb["o"][j], None), x, TM_P, TN_MM,
                        "o_proj")
        act, tail = _ffn_gate_seq(x, wts["f_g_norm"][l], wb["gate"][l], wb["up"][l],
                                  wts["f_w_conv"][l], wts["f_b_conv"][l], TM_P, TN_NORM,
                                  tiles_per_seq, tail_row)
        tail = tail.reshape(bsz, tiles_per_seq, SUBLANES, D_FF)
        ffn_states.append(tail[:, tiles_per_seq - 1, SUBLANES - (FFN_CONV_WIDTH - 1):])
        x = _mm_res(act, (wb["down"][l], None), x, TM_DOWN, TN_MM, "ffn_down")
        if l == N_A_LAYERS - 1:
            k, v, kb, vb = _kv_proj(x, wts["kv_g_norm"], (wb["kv_k"], None), (wb["kv_v"], None), 0,
                                    wts["g_k"], TM_P, TN_NORM, tiles_per_seq, t_real)
            kb = kb.reshape(bsz, T_PAD, D_MODEL)
            vb = vb.reshape(bsz, T_PAD, D_MODEL)
    x = x.reshape(bsz, T_PAD, D_MODEL)[:, N_META:t_real]
    k = k.reshape(bsz, t_real, N_HEADS, HEAD_DIM)
    v = v.reshape(bsz, t_real, N_HEADS, HEAD_DIM)
    return x, jnp.stack(conv_states), jnp.stack(ffn_states), k, v


def _sample_trunk(x, state_conv_a, state_ffn, cache_k, cache_v, page_table, wts):
    nb = x.shape[0]
    m = SAMPLE_ROWS

    def pad_rows(a):
        return jnp.pad(a, ((0, m - nb), (0, 0)))

    x = pad_rows(x.reshape(nb, D_MODEL))
    conv_states, ffn_states = [], []
    k = v = None
    wb = {name: [] for name in ("pw1_a", "pw1_g", "pw2", "gate", "up", "down", "q", "o")}
    half = D_MODEL // TN_MM
    for l in range(DEPTH):
        if l < N_A_LAYERS:
            w1 = (wts["a_w_pw1"], l)
            u, wa, wg = _glu(x, wts["a_g_norm"][l], w1, w1, half, m, TN_MM)
            wb["pw1_a"].append(wa)
            wb["pw1_g"].append(wg)
            u = u[:nb]
            past = state_conv_a[l]
            conv_states.append(jnp.concatenate([past[:, 1:], u[:, None]], axis=1))
            c = _dwconv_step(past, u, wts["a_w_dw"][l], wts["a_b_dw"][l])
            x, w2 = _norm_silu_mm_res(pad_rows(c), wts["a_g_cn"][l], (wts["a_w_pw2"], l), x, m, TN_MM)
            wb["pw2"].append(w2)
        else:
            j = l - N_A_LAYERS
            q, wq = _q_proj(x, wts["b_g_norm"][j], (wts["b_w_q"], j), wts["b_g_q"][j], m, TN_MM)
            wb["q"].append(wq)
            o = _sb_attn_paged(q[:nb], cache_k, cache_v, page_table, wts["b_sb"][j] * LOG2E)
            x, wo = _mm_res(pad_rows(o).astype(BF16), (wts["b_w_o"], j), x, m, TN_MM, "o_proj")
            wb["o"].append(wo)
        past = state_ffn[l]
        act, gate, wgate, wup = _ffn_gate_step(
            x, wts["f_g_norm"][l], (wts["f_w_gate"], l), (wts["f_w_up"], l), wts["f_w_conv"][l],
            wts["f_b_conv"][l], pad_rows(past[:, 0]), pad_rows(past[:, 1]), TN_MM)
        wb["gate"].append(wgate)
        wb["up"].append(wup)
        ffn_states.append(jnp.stack([past[:, 1], gate[:nb]], axis=1))
        x, wdown = _mm_res(act, (wts["f_w_down"], l), x, m, TN_MM, "ffn_down")
        wb["down"].append(wdown)
        if l == N_A_LAYERS - 1:
            wkv = (wts["w_kv"], None)
            k, v, _, _, wb["kv_k"], wb["kv_v"] = _kv_proj(x, wts["kv_g_norm"], wkv, wkv, half,
                                                          wts["g_k"], m, TN_MM, 1, nb)
    y = x[:nb].reshape(nb, 1, D_MODEL)
    k = k.reshape(nb, 1, N_HEADS, HEAD_DIM)
    v = v.reshape(nb, 1, N_HEADS, HEAD_DIM)
    return (y, jnp.stack(conv_states), jnp.stack(ffn_states), k, v), wb


def kernel(x_prompt, x_sample, state_conv_a, state_ffn_conv, cache_k, cache_v, page_table, meta_tokens, a_g_norm, a_w_pw1, a_w_dw, a_b_dw, a_g_cn, a_w_pw2, f_g_norm, f_w_gate, f_w_up, f_w_down, f_w_conv, f_b_conv, kv_g_norm, w_kv, g_k, b_g_norm, b_w_q, b_g_q, b_sb, b_w_o):
    wts = dict(a_g_norm=a_g_norm, a_w_pw1=a_w_pw1, a_w_dw=a_w_dw, a_b_dw=a_b_dw, a_g_cn=a_g_cn,
               a_w_pw2=a_w_pw2, f_g_norm=f_g_norm, f_w_gate=f_w_gate, f_w_up=f_w_up,
               f_w_down=f_w_down, f_w_conv=f_w_conv, f_b_conv=f_b_conv, kv_g_norm=kv_g_norm,
               w_kv=w_kv, g_k=g_k, b_g_norm=b_g_norm, b_w_q=b_w_q, b_g_q=b_g_q, b_sb=b_sb,
               b_w_o=b_w_o)
    (y_s, conv_s, ffn_s, k_s, v_s), wb = _sample_trunk(x_sample, state_conv_a, state_ffn_conv,
                                                       cache_k, cache_v, page_table, wts)
    bp = x_prompt.shape[0]
    meta = jnp.broadcast_to(meta_tokens[None], (bp, N_META, D_MODEL))
    xp = jnp.concatenate([meta, x_prompt], axis=1)
    y_p, conv_p, ffn_p, k_p, v_p = _prompt_trunk(xp, wts, wb)
    return (y_p, y_s, conv_p, ffn_p, k_p, v_p, conv_s, ffn_s, k_s, v_s)
```

```python
import functools

import jax
import jax.numpy as jnp
from jax import lax
from jax.experimental import pallas as pl
from jax.experimental.pallas import tpu as pltpu

D_MODEL = 2048
D_FF = 5632
N_HEADS = 16
HEAD_DIM = 128
N_META = 16
N_A_LAYERS = 2
DEPTH = 4
CONV_A_WIDTH = 31
FFN_CONV_WIDTH = 3
PAGE_SIZE = 128
EPS = 1e-6

LANES = 128
SUBLANES = 8
VMEM_LIMIT = 56 * 1024 * 1024

F32 = jnp.float32
BF16 = jnp.bfloat16


def _cparams(sem):
    return pltpu.CompilerParams(dimension_semantics=sem, vmem_limit_bytes=VMEM_LIMIT)


def _rms_rows(x, g):
    ms = jnp.mean(x * x, axis=-1, keepdims=True)
    return x * lax.rsqrt(ms + EPS) * g


def _head_rms(t, gh):
    parts = []
    for s in range(t.shape[1] // HEAD_DIM):
        parts.append(_rms_rows(t[:, s * HEAD_DIM:(s + 1) * HEAD_DIM], gh))
    return parts


LOG2E = 1.4426950408889634

Q_SCALE = HEAD_DIM ** -0.5 * LOG2E


def _softplus2(z2):
    sign = jnp.uint32(0x80000000)
    neg_abs = lax.bitcast_convert_type(lax.bitcast_convert_type(z2, jnp.uint32) | sign, F32)
    return jnp.maximum(z2, 0.0) + jnp.log(1.0 + jnp.exp2(neg_abs)) * LOG2E


ROW_CHUNK = 272


def _row_chunks(tm):
    rc = ROW_CHUNK if tm % ROW_CHUNK == 0 else tm
    return [slice(r0, r0 + rc) for r0 in range(0, tm, rc)]


def _bf16_weights(w_ref, wb_ref):
    if wb_ref is None:
        return w_ref
    wb_ref[...] = w_ref[...].astype(BF16)
    return wb_ref


def _weight_outs(emit, n_w, k, n, tn):
    if not emit:
        return [], []
    spec = pl.BlockSpec((k, tn), lambda *ij: (0, ij[-1]))
    return [spec] * n_w, [jax.ShapeDtypeStruct((k, n), BF16)] * n_w


def _staged_chunks(h_scr, stage, consume):
    chunks = _row_chunks(h_scr.shape[0])

    @pl.when(pl.program_id(1) == 0)
    def _():
        for rows in chunks:
            h_scr[rows, :] = stage(rows)
            consume(rows)

    @pl.when(pl.program_id(1) != 0)
    def _():
        for rows in chunks:
            consume(rows)


def _glu_kernel(x_ref, g_ref, wa_ref, wg_ref, u_ref, *rest, emit):
    wa_o, wg_o, h_scr = rest if emit else (None, None, *rest)
    wa_ref, wg_ref = _bf16_weights(wa_ref, wa_o), _bf16_weights(wg_ref, wg_o)

    def consume(rows):
        h = h_scr[rows, :]
        a = jnp.dot(h, wa_ref[...], preferred_element_type=F32)
        gt = jnp.dot(h, wg_ref[...], preferred_element_type=F32)
        u_ref[rows, :] = a * jax.nn.sigmoid(gt)

    _staged_chunks(h_scr, lambda rows: _rms_rows(x_ref[rows, :], g_ref[...]).astype(BF16), consume)


def _w_spec(w, tn, col0=0):
    arr, layer = w
    k = arr.shape[-2]
    if layer is None:
        return pl.BlockSpec((k, tn), lambda *ij: (0, ij[-1] + col0))
    return pl.BlockSpec((None, k, tn), lambda *ij: (layer, 0, ij[-1] + col0))


def _glu(x, g, w_a, w_g, g_col0, tm, tn):
    m = x.shape[0]
    emit = w_a[0].dtype != BF16
    w_specs, w_shapes = _weight_outs(emit, 2, D_MODEL, D_MODEL, tn)
    out = pl.pallas_call(
        functools.partial(_glu_kernel, emit=emit),
        grid=(m // tm, D_MODEL // tn),
        in_specs=[
            pl.BlockSpec((tm, D_MODEL), lambda i, j: (i, 0)),
            pl.BlockSpec((1, D_MODEL), lambda i, j: (0, 0)),
            _w_spec(w_a, tn),
            _w_spec(w_g, tn, g_col0),
        ],
        out_specs=[pl.BlockSpec((tm, tn), lambda i, j: (i, j))] + w_specs,
        out_shape=[jax.ShapeDtypeStruct((m, D_MODEL), F32)] + w_shapes,
        scratch_shapes=[pltpu.VMEM((tm, D_MODEL), BF16)],
        compiler_params=_cparams(("parallel", "arbitrary")),
        name="glu",
    )(x, g.reshape(1, D_MODEL), w_a[0], w_g[0])
    return out if emit else out[0]


DW_HALO = 32


def _dwconv_seq_kernel(u_ref, halo_ref, w_ref, b_ref, o_ref, full_scr, sh_scr, *, tt, cl):
    first = pl.program_id(1) == 0
    full_scr[0:DW_HALO, :] = jnp.where(first, 0.0, halo_ref[0])
    full_scr[DW_HALO:DW_HALO + tt, :] = u_ref[0]
    lead = DW_HALO - (CONV_A_WIDTH - 1)
    sh_rows = sh_scr.shape[1]
    for c in range(cl // LANES):
        ls = slice(c * LANES, (c + 1) * LANES)
        for s in range(1, SUBLANES):
            sh_scr[s - 1, :, ls] = full_scr[s:s + sh_rows, ls]
        acc = jnp.broadcast_to(b_ref[:, ls], (tt, LANES))
        for k in range(CONV_A_WIDTH):
            s, a0 = (lead + k) % SUBLANES, (lead + k) // SUBLANES * SUBLANES
            rows = full_scr[a0:a0 + tt, ls] if s == 0 else sh_scr[s - 1, a0:a0 + tt, ls]
            acc = acc + w_ref[k:k + 1, ls] * rows
        o_ref[0, :, ls] = acc


def _dwconv_seq(u3, w, b, tt=256, cl=1024):
    bsz, tp, _ = u3.shape
    w_pad = jnp.zeros((DW_HALO, D_MODEL), F32).at[:CONV_A_WIDTH].set(w)
    hb = tt // DW_HALO
    return pl.pallas_call(
        functools.partial(_dwconv_seq_kernel, tt=tt, cl=cl),
        grid=(bsz, tp // tt, D_MODEL // cl),
        in_specs=[
            pl.BlockSpec((1, tt, cl), lambda bb, i, c: (bb, i, c)),
            pl.BlockSpec((1, DW_HALO, cl), lambda bb, i, c: (bb, jnp.maximum(i * hb - 1, 0), c)),
            pl.BlockSpec((DW_HALO, cl), lambda bb, i, c: (0, c)),
            pl.BlockSpec((1, cl), lambda bb, i, c: (0, c)),
        ],
        out_specs=pl.BlockSpec((1, tt, cl), lambda bb, i, c: (bb, i, c)),
        out_shape=jax.ShapeDtypeStruct(u3.shape, F32),
        scratch_shapes=[pltpu.VMEM((DW_HALO + tt, cl), F32),
                        pltpu.VMEM((SUBLANES - 1, DW_HALO + tt - SUBLANES, cl), F32)],
        compiler_params=_cparams(("parallel", "parallel", "parallel")),
        name="dwconv_seq",
    )(u3, u3, w_pad, b.reshape(1, D_MODEL))


def _dwconv_step_kernel(s_ref, u_ref, w_ref, b_ref, o_ref):
    w = w_ref[...]
    past = jnp.sum(s_ref[...] * w[None, :CONV_A_WIDTH - 1, :], axis=1)
    o_ref[...] = past + u_ref[...] * w[CONV_A_WIDTH - 1:CONV_A_WIDTH, :] + b_ref[...]


def _dwconv_step(state, u, w, b):
    nb = state.shape[0]
    return pl.pallas_call(
        _dwconv_step_kernel,
        out_shape=jax.ShapeDtypeStruct((nb, D_MODEL), F32),
        compiler_params=pltpu.CompilerParams(vmem_limit_bytes=VMEM_LIMIT),
        name="dwconv_step",
    )(state, u, w, b.reshape(1, D_MODEL))


def _norm_silu_mm_res_kernel(c_ref, g_ref, w_ref, x_ref, o_ref, *rest, emit):
    w_o, h_scr = rest if emit else (None, *rest)
    w_ref = _bf16_weights(w_ref, w_o)

    def stage(rows):
        y = _rms_rows(c_ref[rows, :], g_ref[...])
        return (y * jax.nn.sigmoid(y)).astype(BF16)

    def consume(rows):
        o_ref[rows, :] = x_ref[rows, :] + jnp.dot(h_scr[rows, :], w_ref[...],
                                                  preferred_element_type=F32)

    _staged_chunks(h_scr, stage, consume)


def _norm_silu_mm_res(c, g, w, xres, tm, tn):
    m = c.shape[0]
    emit = w[0].dtype != BF16
    w_specs, w_shapes = _weight_outs(emit, 1, D_MODEL, D_MODEL, tn)
    out = pl.pallas_call(
        functools.partial(_norm_silu_mm_res_kernel, emit=emit),
        grid=(m // tm, D_MODEL // tn),
        in_specs=[
            pl.BlockSpec((tm, D_MODEL), lambda i, j: (i, 0)),
            pl.BlockSpec((1, D_MODEL), lambda i, j: (0, 0)),
            _w_spec(w, tn),
            pl.BlockSpec((tm, tn), lambda i, j: (i, j)),
        ],
        out_specs=[pl.BlockSpec((tm, tn), lambda i, j: (i, j))] + w_specs,
        out_shape=[jax.ShapeDtypeStruct((m, D_MODEL), F32)] + w_shapes,
        scratch_shapes=[pltpu.VMEM((tm, D_MODEL), BF16)],
        compiler_params=_cparams(("parallel", "arbitrary")),
        name="pw2",
    )(c, g.reshape(1, D_MODEL), w[0], xres)
    return out if emit else out[0]


def _ffn_gate_seq_kernel(x_ref, xp_ref, g_ref, wg_ref, wu_ref, wc_ref, bc_ref,
                         act_ref, tail_ref, h_scr, hp_scr, gs_scr, *, tm, tiles_per_seq, tail_row):
    i = pl.program_id(0)

    @pl.when(pl.program_id(1) == 0)
    def _():
        h_scr[...] = _rms_rows(x_ref[...], g_ref[...]).astype(BF16)
        hp = _rms_rows(xp_ref[...], g_ref[...])
        hp = jnp.where(i % tiles_per_seq == 0, 0.0, hp)
        hp_scr[...] = jnp.concatenate([hp, hp], axis=0).astype(BF16)

    gprev = jnp.dot(hp_scr[...], wg_ref[...], preferred_element_type=F32)
    gs_scr[0:SUBLANES, :] = gprev[0:SUBLANES]

    def consume(rows):
        r0, r1 = rows.start, rows.stop
        h = h_scr[rows, :]
        gate = jnp.dot(h, wg_ref[...], preferred_element_type=F32)
        up = jnp.dot(h, wu_ref[...], preferred_element_type=F32)
        gs_scr[SUBLANES + r0:SUBLANES + r1, :] = gate
        g1 = gs_scr[SUBLANES - 1 + r0:SUBLANES - 1 + r1, :]
        g2 = gs_scr[SUBLANES - 2 + r0:SUBLANES - 2 + r1, :]
        gc = wc_ref[0:1, :] * g2 + wc_ref[1:2, :] * g1 + wc_ref[2:3, :] * gate + bc_ref[...]
        act_ref[rows, :] = (gc * jax.nn.sigmoid(gc) * up).astype(BF16)
        if r0 <= tail_row < r1:
            tail_ref[0] = gate[tail_row - r0:tail_row - r0 + SUBLANES]

    for rows in _row_chunks(tm):
        consume(rows)


def _ffn_gate_seq(x, g, wg_bf, wu_bf, wc, bc, tm, tn, tiles_per_seq, tail_row):
    m = x.shape[0]
    nt = m // tm
    hb = tm // SUBLANES
    wc_pad = jnp.zeros((SUBLANES, D_FF), F32).at[:FFN_CONV_WIDTH].set(wc)
    return pl.pallas_call(
        functools.partial(_ffn_gate_seq_kernel, tm=tm, tiles_per_seq=tiles_per_seq, tail_row=tail_row),
        grid=(nt, D_FF // tn),
        in_specs=[
            pl.BlockSpec((tm, D_MODEL), lambda i, j: (i, 0)),
            pl.BlockSpec((SUBLANES, D_MODEL), lambda i, j: (jnp.maximum(i * hb - 1, 0), 0)),
            pl.BlockSpec((1, D_MODEL), lambda i, j: (0, 0)),
            pl.BlockSpec((D_MODEL, tn), lambda i, j: (0, j)),
            pl.BlockSpec((D_MODEL, tn), lambda i, j: (0, j)),
            pl.BlockSpec((SUBLANES, tn), lambda i, j: (0, j)),
            pl.BlockSpec((1, tn), lambda i, j: (0, j)),
        ],
        out_specs=[
            pl.BlockSpec((tm, tn), lambda i, j: (i, j)),
            pl.BlockSpec((1, SUBLANES, tn), lambda i, j: (i, 0, j)),
        ],
        out_shape=[
            jax.ShapeDtypeStruct((m, D_FF), BF16),
            jax.ShapeDtypeStruct((nt, SUBLANES, D_FF), F32),
        ],
        scratch_shapes=[
            pltpu.VMEM((tm, D_MODEL), BF16),
            pltpu.VMEM((2 * SUBLANES, D_MODEL), BF16),
            pltpu.VMEM((SUBLANES + tm, tn), F32),
        ],
        compiler_params=_cparams(("parallel", "arbitrary")),
        name="ffn_gate_seq",
    )(x, x, g.reshape(1, D_MODEL), wg_bf, wu_bf, wc_pad, bc.reshape(1, D_FF))


def _ffn_gate_step_kernel(x_ref, g_ref, wg_ref, wu_ref, wc_ref, bc_ref, s0_ref, s1_ref,
                          act_ref, gate_ref, wg_o, wu_o, h_scr):
    @pl.when(pl.program_id(0) == 0)
    def _():
        h_scr[...] = _rms_rows(x_ref[...], g_ref[...]).astype(BF16)

    h = h_scr[...]
    gate = jnp.dot(h, _bf16_weights(wg_ref, wg_o)[...], preferred_element_type=F32)
    up = jnp.dot(h, _bf16_weights(wu_ref, wu_o)[...], preferred_element_type=F32)
    gc = (wc_ref[0:1, :] * s0_ref[...] + wc_ref[1:2, :] * s1_ref[...]
          + wc_ref[2:3, :] * gate + bc_ref[...])
    act_ref[...] = (gc * jax.nn.sigmoid(gc) * up).astype(BF16)
    gate_ref[...] = gate


def _ffn_gate_step(x, g, wg, wu, wc, bc, s0, s1, tn):
    m = x.shape[0]
    wc_pad = jnp.zeros((SUBLANES, D_FF), F32).at[:FFN_CONV_WIDTH].set(wc)
    w_specs, w_shapes = _weight_outs(True, 2, D_MODEL, D_FF, tn)
    return pl.pallas_call(
        _ffn_gate_step_kernel,
        grid=(D_FF // tn,),
        in_specs=[
            pl.BlockSpec((m, D_MODEL), lambda j: (0, 0)),
            pl.BlockSpec((1, D_MODEL), lambda j: (0, 0)),
            _w_spec(wg, tn),
            _w_spec(wu, tn),
            pl.BlockSpec((SUBLANES, tn), lambda j: (0, j)),
            pl.BlockSpec((1, tn), lambda j: (0, j)),
            pl.BlockSpec((m, tn), lambda j: (0, j)),
            pl.BlockSpec((m, tn), lambda j: (0, j)),
        ],
        out_specs=[
            pl.BlockSpec((m, tn), lambda j: (0, j)),
            pl.BlockSpec((m, tn), lambda j: (0, j)),
        ] + w_specs,
        out_shape=[
            jax.ShapeDtypeStruct((m, D_FF), BF16),
            jax.ShapeDtypeStruct((m, D_FF), F32),
        ] + w_shapes,
        scratch_shapes=[pltpu.VMEM((m, D_MODEL), BF16)],
        compiler_params=_cparams(("arbitrary",)),
        name="ffn_gate_step",
    )(x, g.reshape(1, D_MODEL), wg[0], wu[0], wc_pad, bc.reshape(1, D_FF), s0, s1)


def _mm_res_kernel(a_ref, w_ref, x_ref, o_ref, w_o=None):
    w_ref = _bf16_weights(w_ref, w_o)
    for rows in _row_chunks(o_ref.shape[0]):
        o_ref[rows, :] = x_ref[rows, :] + jnp.dot(a_ref[rows, :], w_ref[...],
                                                  preferred_element_type=F32)


def _mm_res(a_bf, w, xres, tm, tn, name):
    m, kdim = a_bf.shape
    n = w[0].shape[-1]
    emit = w[0].dtype != BF16
    w_specs, w_shapes = _weight_outs(emit, 1, kdim, n, tn)
    out = pl.pallas_call(
        _mm_res_kernel,
        grid=(m // tm, n // tn),
        in_specs=[
            pl.BlockSpec((tm, kdim), lambda i, j: (i, 0)),
            _w_spec(w, tn),
            pl.BlockSpec((tm, tn), lambda i, j: (i, j)),
        ],
        out_specs=[pl.BlockSpec((tm, tn), lambda i, j: (i, j))] + w_specs,
        out_shape=[jax.ShapeDtypeStruct((m, n), F32)] + w_shapes,
        compiler_params=_cparams(("parallel", "parallel")),
        name=name,
    )(a_bf, w[0], xres)
    return out if emit else out[0]


def _kv_kernel(x_ref, g_ref, wk_ref, wv_ref, gk_ref, k_ref, v_ref, kb_ref, vb_ref, *rest, emit):
    wk_o, wv_o, h_scr = rest if emit else (None, None, *rest)
    wk_ref, wv_ref = _bf16_weights(wk_ref, wk_o), _bf16_weights(wv_ref, wv_o)

    def consume(rows):
        h = h_scr[rows, :]
        kraw = jnp.dot(h, wk_ref[...], preferred_element_type=F32)
        v = jnp.dot(h, wv_ref[...], preferred_element_type=F32)
        for s, kn in enumerate(_head_rms(kraw, gk_ref[...])):
            hc = slice(s * HEAD_DIM, (s + 1) * HEAD_DIM)
            k_ref[rows, hc] = kn
            kb_ref[rows, hc] = kn.astype(BF16)
        v_ref[rows, :] = v
        vb_ref[rows, :] = v.astype(BF16)

    _staged_chunks(h_scr, lambda rows: _rms_rows(x_ref[rows, :], g_ref[...]).astype(BF16), consume)


def _kv_proj(x, g, w_k, w_v, v_col0, gk, tm, tn, tiles_per_seq, t_out):
    m = x.shape[0]
    bsz = m // (tm * tiles_per_seq)
    emit = w_k[0].dtype != BF16
    w_specs, w_shapes = _weight_outs(emit, 2, D_MODEL, D_MODEL, tn)
    blk = pl.BlockSpec((tm, tn), lambda i, j: (i, j))
    blk_seq = pl.BlockSpec((None, tm, tn), lambda i, j: (i // tiles_per_seq, i % tiles_per_seq, j))
    return pl.pallas_call(
        functools.partial(_kv_kernel, emit=emit),
        grid=(m // tm, D_MODEL // tn),
        in_specs=[
            pl.BlockSpec((tm, D_MODEL), lambda i, j: (i, 0)),
            pl.BlockSpec((1, D_MODEL), lambda i, j: (0, 0)),
            _w_spec(w_k, tn),
            _w_spec(w_v, tn, v_col0),
            pl.BlockSpec((1, HEAD_DIM), lambda i, j: (0, 0)),
        ],
        out_specs=[blk_seq, blk_seq, blk, blk] + w_specs,
        out_shape=[
            jax.ShapeDtypeStruct((bsz, t_out, D_MODEL), F32),
            jax.ShapeDtypeStruct((bsz, t_out, D_MODEL), F32),
            jax.ShapeDtypeStruct((m, D_MODEL), BF16),
            jax.ShapeDtypeStruct((m, D_MODEL), BF16),
        ] + w_shapes,
        scratch_shapes=[pltpu.VMEM((tm, D_MODEL), BF16)],
        compiler_params=_cparams(("parallel", "arbitrary")),
        name="kv_proj",
    )(x, g.reshape(1, D_MODEL), w_k[0], w_v[0], gk.reshape(1, HEAD_DIM))


def _q_kernel(x_ref, g_ref, w_ref, gq_ref, q_ref, *rest, emit):
    w_o, h_scr = rest if emit else (None, *rest)
    w_ref = _bf16_weights(w_ref, w_o)

    def consume(rows):
        qraw = jnp.dot(h_scr[rows, :], w_ref[...], preferred_element_type=F32)
        for s, qn in enumerate(_head_rms(qraw, gq_ref[...])):
            q_ref[rows, s * HEAD_DIM:(s + 1) * HEAD_DIM] = (qn * Q_SCALE).astype(BF16)

    _staged_chunks(h_scr, lambda rows: _rms_rows(x_ref[rows, :], g_ref[...]).astype(BF16), consume)


def _q_proj(x, g, w, gq, tm, tn):
    m = x.shape[0]
    emit = w[0].dtype != BF16
    w_specs, w_shapes = _weight_outs(emit, 1, D_MODEL, D_MODEL, tn)
    out = pl.pallas_call(
        functools.partial(_q_kernel, emit=emit),
        grid=(m // tm, D_MODEL // tn),
        in_specs=[
            pl.BlockSpec((tm, D_MODEL), lambda i, j: (i, 0)),
            pl.BlockSpec((1, D_MODEL), lambda i, j: (0, 0)),
            _w_spec(w, tn),
            pl.BlockSpec((1, HEAD_DIM), lambda i, j: (0, 0)),
        ],
        out_specs=[pl.BlockSpec((tm, tn), lambda i, j: (i, j))] + w_specs,
        out_shape=[jax.ShapeDtypeStruct((m, D_MODEL), BF16)] + w_shapes,
        scratch_shapes=[pltpu.VMEM((tm, D_MODEL), BF16)],
        compiler_params=_cparams(("parallel", "arbitrary")),
        name="q_proj",
    )(x, g.reshape(1, D_MODEL), w[0], gq.reshape(1, HEAD_DIM))
    return out if emit else out[0]


NT_DIMS = (((1,), (1,)), ((), ()))


def _neg_suffix(tk):
    r = lax.broadcasted_iota(jnp.int32, (tk, tk), 0)
    c = lax.broadcasted_iota(jnp.int32, (tk, tk), 1)
    return -(r >= c).astype(BF16)


def _neg_suffix_and_total(tk):
    return jnp.concatenate([_neg_suffix(tk), -jnp.ones((tk, tk), BF16)], axis=1)


def _sb_attn_seq_kernel(sb_ref, q_ref, k_ref, v_ref, nst_ref, o_ref, acc_scr, run_scr,
                        *, tq, tk, hs, n_last):
    hg = pl.program_id(1)
    qi = pl.program_id(2)
    sub = tq // tk
    nst = nst_ref[...]
    lanes = [slice(s * HEAD_DIM, (s + 1) * HEAD_DIM) for s in range(hs)]

    def logits(j, r0, nr):
        off = pl.multiple_of(j * tk, tk)
        return [lax.dot_general(q_ref[0, r0:nr, lanes[s]], k_ref[0, pl.ds(off, tk), lanes[s]],
                                NT_DIMS, preferred_element_type=F32) + sb_ref[hg * hs + s]
                for s in range(hs)]

    def block(j, r0, nr, masked):
        rows = nr - r0
        off = pl.multiple_of(j * tk, tk)
        zs = logits(j, r0, nr)
        if masked:
            diff = (lax.broadcasted_iota(jnp.int32, (rows, tk), 0)
                    - lax.broadcasted_iota(jnp.int32, (rows, tk), 1))
            mask = diff > (j * tk - qi * tq - r0)
        css = []
        for s in range(hs):
            sp = _softplus2(zs[s])
            if masked:
                sp = jnp.where(mask, sp, 0.0)
            css.append(jnp.dot(sp.astype(BF16), nst, preferred_element_type=F32))
        for s in range(hs):
            run = run_scr[s, r0:nr, :]
            att = jnp.exp2(zs[s] + css[s] + jnp.concatenate([run] * (tk // LANES), axis=1))
            if masked:
                att = jnp.where(mask, att, 0.0)
            vs = v_ref[0, pl.ds(off, tk), lanes[s]]
            acc_scr[s, r0:nr, :] += jnp.dot(att.astype(BF16), vs, preferred_element_type=F32)
            run_scr[s, r0:nr, :] = run + jnp.broadcast_to(css[s][:, 0:1], (rows, LANES))

    def tile(nr):
        acc_scr[:, 0:nr, :] = jnp.zeros((hs, nr, HEAD_DIM), F32)
        run_scr[:, 0:nr, :] = jnp.zeros((hs, nr, LANES), F32)
        for d in reversed(range(sub)):
            if d * tk < nr:
                block(qi * sub + d, d * tk, nr, True)

        def body(jj, carry):
            block(qi * sub - 1 - jj, 0, nr, False)
            return carry

        lax.fori_loop(0, qi * sub, body, 0)
        for s in range(hs):
            o_ref[0, 0:nr, lanes[s]] = acc_scr[s, 0:nr, :].astype(BF16)
        if nr < tq:
            o_ref[0, nr:tq, :] = jnp.zeros((tq - nr, hs * HEAD_DIM), BF16)

    if n_last == tq:
        tile(tq)
    else:
        last = pl.num_programs(2) - 1
        pl.when(qi != last)(lambda: tile(tq))
        pl.when(qi == last)(lambda: tile(n_last))


def _sb_attn_seq(q3, kb3, vb3, sb2, t_real, tq=256, tk=256, hs=8):
    bsz, tp, _ = q3.shape
    wl = hs * HEAD_DIM
    n_last = t_real - (tp // tq - 1) * tq
    assert 0 < n_last <= tq and n_last % (2 * SUBLANES) == 0
    grid_spec = pltpu.PrefetchScalarGridSpec(
        num_scalar_prefetch=1,
        grid=(bsz, N_HEADS // hs, tp // tq),
        in_specs=[
            pl.BlockSpec((1, tq, wl), lambda b, h, i, sbr: (b, i, h)),
            pl.BlockSpec((1, tp, wl), lambda b, h, i, sbr: (b, 0, h)),
            pl.BlockSpec((1, tp, wl), lambda b, h, i, sbr: (b, 0, h)),
            pl.BlockSpec((tk, tk), lambda b, h, i, sbr: (0, 0)),
        ],
        out_specs=pl.BlockSpec((1, tq, wl), lambda b, h, i, sbr: (b, i, h)),
        scratch_shapes=[pltpu.VMEM((hs, tq, HEAD_DIM), F32), pltpu.VMEM((hs, tq, LANES), F32)],
    )
    return pl.pallas_call(
        functools.partial(_sb_attn_seq_kernel, tq=tq, tk=tk, hs=hs, n_last=n_last),
        grid_spec=grid_spec,
        out_shape=jax.ShapeDtypeStruct(q3.shape, BF16),
        compiler_params=_cparams(("parallel", "parallel", "arbitrary")),
        name="sb_attn_seq",
    )(sb2, q3, kb3, vb3, _neg_suffix(tk))


PAGE_ROWS = PAGE_SIZE * N_HEADS
PAGES_PER_STEP = 8


def _sb_attn_paged_kernel(pt_ref, q_ref, bias_ref, *refs, n_steps, pp):
    k_refs, v_refs = refs[:pp], refs[pp:2 * pp]
    nst_ref, o_ref, acc_scr, run_scr = refs[2 * pp:]
    step = pl.program_id(1)

    @pl.when(step == 0)
    def _():
        acc_scr[...] = jnp.zeros_like(acc_scr)
        run_scr[...] = jnp.zeros_like(run_scr)

    n_tiles = PAGE_ROWS // LANES
    q = q_ref[0]
    bias = jnp.concatenate([bias_ref[...]] * n_tiles, axis=1)
    own = ((lax.broadcasted_iota(jnp.int32, (N_HEADS, PAGE_ROWS), 1) & (N_HEADS - 1))
           == lax.broadcasted_iota(jnp.int32, (N_HEADS, PAGE_ROWS), 0))
    nst = nst_ref[...]
    run = run_scr[...]
    acc = acc_scr[...]
    zs = []
    for i in range(pp):
        kb = k_refs[i][0].astype(BF16)
        zs.append(lax.dot_general(q, kb, NT_DIMS, preferred_element_type=F32) + bias)
    css = []
    for i in range(pp):
        sp = jnp.where(own, _softplus2(zs[i]), 0.0)
        sp_st = jnp.concatenate([sp[:, t * LANES:(t + 1) * LANES] for t in range(n_tiles)], axis=0)
        css.append(jnp.dot(sp_st.astype(BF16), nst, preferred_element_type=F32))
    for i in range(pp):
        logw = [None] * n_tiles
        for t in reversed(range(n_tiles)):
            rows = slice(t * N_HEADS, (t + 1) * N_HEADS)
            logw[t] = zs[i][:, t * LANES:(t + 1) * LANES] + css[i][rows, :LANES] + run
            run = run + css[i][rows, LANES:]
        att = jnp.where(own, jnp.exp2(jnp.concatenate(logw, axis=1)), 0.0)
        vb = v_refs[i][0].astype(BF16)
        acc = acc + jnp.dot(att.astype(BF16), vb, preferred_element_type=F32)
    acc_scr[...] = acc
    run_scr[...] = run

    @pl.when(step == n_steps - 1)
    def _():
        o_ref[0] = acc


def _sb_attn_paged(q, cache_k, cache_v, page_table, sb2):
    assert N_HEADS & (N_HEADS - 1) == 0 and LANES % N_HEADS == 0
    nb, n_pages = page_table.shape
    n_pool = cache_k.shape[0]
    pp = PAGES_PER_STEP
    n_steps = n_pages // pp
    assert n_steps * pp == n_pages
    ck = cache_k.reshape(n_pool, PAGE_ROWS, HEAD_DIM)
    cv = cache_v.reshape(n_pool, PAGE_ROWS, HEAD_DIM)
    q3 = q.reshape(nb, N_HEADS, HEAD_DIM)
    bias = jnp.broadcast_to(sb2.astype(F32)[:, None], (N_HEADS, LANES))

    def page_spec(i):
        return pl.BlockSpec((1, PAGE_ROWS, HEAD_DIM),
                            lambda b, s, pt: (pt[b, n_pages - 1 - (s * pp + i)], 0, 0))

    grid_spec = pltpu.PrefetchScalarGridSpec(
        num_scalar_prefetch=1,
        grid=(nb, n_steps),
        in_specs=[
            pl.BlockSpec((1, N_HEADS, HEAD_DIM), lambda b, s, pt: (b, 0, 0)),
            pl.BlockSpec((N_HEADS, LANES), lambda b, s, pt: (0, 0)),
            *[page_spec(i) for i in range(pp)],
            *[page_spec(i) for i in range(pp)],
            pl.BlockSpec((LANES, 2 * LANES), lambda b, s, pt: (0, 0)),
        ],
        out_specs=pl.BlockSpec((1, N_HEADS, HEAD_DIM), lambda b, s, pt: (b, 0, 0)),
        scratch_shapes=[pltpu.VMEM((N_HEADS, HEAD_DIM), F32), pltpu.VMEM((N_HEADS, LANES), F32)],
    )
    o = pl.pallas_call(
        functools.partial(_sb_attn_paged_kernel, n_steps=n_steps, pp=pp),
        grid_spec=grid_spec,
        out_shape=jax.ShapeDtypeStruct((nb, N_HEADS, HEAD_DIM), F32),
        compiler_params=_cparams(("parallel", "arbitrary")),
        name="sb_attn_paged",
    )(page_table, q3, bias, *([ck] * pp), *([cv] * pp), _neg_suffix_and_total(LANES))
    return o.reshape(nb, D_MODEL)


T_PAD = 4352
TM_P = 1088
TM_DOWN = 1088
TN_NORM = 512
TN_MM = 512
SAMPLE_ROWS = 16


def _prompt_trunk(x, wts, wb):
    bsz, t_real, _ = x.shape
    pad = T_PAD - t_real
    x = jnp.pad(x, ((0, 0), (0, pad), (0, 0))).reshape(bsz * T_PAD, D_MODEL)
    tiles_per_seq = T_PAD // TM_P
    tail_row = (t_real - 1) % TM_P - (SUBLANES - 1)
    assert (t_real - 1) // TM_P == tiles_per_seq - 1 and tail_row % SUBLANES == 0
    conv_states, ffn_states = [], []
    k = v = kb = vb = None
    for l in range(DEPTH):
        if l < N_A_LAYERS:
            u = _glu(x, wts["a_g_norm"][l], (wb["pw1_a"][l], None), (wb["pw1_g"][l], None), 0,
                     TM_P, TN_NORM)
            u3 = u.reshape(bsz, T_PAD, D_MODEL)
            conv_states.append(u3[:, t_real - (CONV_A_WIDTH - 1):t_real])
            c = _dwconv_seq(u3, wts["a_w_dw"][l], wts["a_b_dw"][l])
            x = _norm_silu_mm_res(c.reshape(bsz * T_PAD, D_MODEL), wts["a_g_cn"][l],
                                  (wb["pw2"][l], None), x, TM_P, TN_NORM)
        else:
            j = l - N_A_LAYERS
            q = _q_proj(x, wts["b_g_norm"][j], (wb["q"][j], None), wts["b_g_q"][j], TM_P, TN_NORM)
            o = _sb_attn_seq(q.reshape(bsz, T_PAD, D_MODEL), kb, vb, wts["b_sb"][j] * LOG2E, t_real)
            x = _mm_res(o.reshape(bsz * T_PAD, D_MODEL), (wb["o"][j], None), x, TM_P, TN_MM,
                        "o_proj")
        act, tail = _ffn_gate_seq(x, wts["f_g_norm"][l], wb["gate"][l], wb["up"][l],
                                  wts["f_w_conv"][l], wts["f_b_conv"][l], TM_P, TN_NORM,
                                  tiles_per_seq, tail_row)
        tail = tail.reshape(bsz, tiles_per_seq, SUBLANES, D_FF)
        ffn_states.append(tail[:, tiles_per_seq - 1, SUBLANES - (FFN_CONV_WIDTH - 1):])
        x = _mm_res(act, (wb["down"][l], None), x, TM_DOWN, TN_MM, "ffn_down")
        if l == N_A_LAYERS - 1:
            k, v, kb, vb = _kv_proj(x, wts["kv_g_norm"], (wb["kv_k"], None), (wb["kv_v"], None), 0,
                                    wts["g_k"], TM_P, TN_NORM, tiles_per_seq, t_real)
            kb = kb.reshape(bsz, T_PAD, D_MODEL)
            vb = vb.reshape(bsz, T_PAD, D_MODEL)
    x = x.reshape(bsz, T_PAD, D_MODEL)[:, N_META:t_real]
    k = k.reshape(bsz, t_real, N_HEADS, HEAD_DIM)
    v = v.reshape(bsz, t_real, N_HEADS, HEAD_DIM)
    return x, jnp.stack(conv_states), jnp.stack(ffn_states), k, v


def _sample_trunk(x, state_conv_a, state_ffn, cache_k, cache_v, page_table, wts):
    nb = x.shape[0]
    m = SAMPLE_ROWS

    def pad_rows(a):
        return jnp.pad(a, ((0, m - nb), (0, 0)))

    x = pad_rows(x.reshape(nb, D_MODEL))
    conv_states, ffn_states = [], []
    k = v = None
    wb = {name: [] for name in ("pw1_a", "pw1_g", "pw2", "gate", "up", "down", "q", "o")}
    half = D_MODEL // TN_MM
    for l in range(DEPTH):
        if l < N_A_LAYERS:
            w1 = (wts["a_w_pw1"], l)
            u, wa, wg = _glu(x, wts["a_g_norm"][l], w1, w1, half, m, TN_MM)
            wb["pw1_a"].append(wa)
            wb["pw1_g"].append(wg)
            u = u[:nb]
            past = state_conv_a[l]
            conv_states.append(jnp.concatenate([past[:, 1:], u[:, None]], axis=1))
            c = _dwconv_step(past, u, wts["a_w_dw"][l], wts["a_b_dw"][l])
            x, w2 = _norm_silu_mm_res(pad_rows(c), wts["a_g_cn"][l], (wts["a_w_pw2"], l), x, m, TN_MM)
            wb["pw2"].append(w2)
        else:
            j = l - N_A_LAYERS
            q, wq = _q_proj(x, wts["b_g_norm"][j], (wts["b_w_q"], j), wts["b_g_q"][j], m, TN_MM)
            wb["q"].append(wq)
            o = _sb_attn_paged(q[:nb], cache_k, cache_v, page_table, wts["b_sb"][j] * LOG2E)
            x, wo = _mm_res(pad_rows(o).astype(BF16), (wts["b_w_o"], j), x, m, TN_MM, "o_proj")
            wb["o"].append(wo)
        past = state_ffn[l]
        act, gate, wgate, wup = _ffn_gate_step(
            x, wts["f_g_norm"][l], (wts["f_w_gate"], l), (wts["f_w_up"], l), wts["f_w_conv"][l],
            wts["f_b_conv"][l], pad_rows(past[:, 0]), pad_rows(past[:, 1]), TN_MM)
        wb["gate"].append(wgate)
        wb["up"].append(wup)
        ffn_states.append(jnp.stack([past[:, 1], gate[:nb]], axis=1))
        x, wdown = _mm_res(act, (wts["f_w_down"], l), x, m, TN_MM, "ffn_down")
        wb["down"].append(wdown)
        if l == N_A_LAYERS - 1:
            wkv = (wts["w_kv"], None)
            k, v, _, _, wb["kv_k"], wb["kv_v"] = _kv_proj(x, wts["kv_g_norm"], wkv, wkv, half,
                                                          wts["g_k"], m, TN_MM, 1, nb)
    y = x[:nb].reshape(nb, 1, D_MODEL)
    k = k.reshape(nb, 1, N_HEADS, HEAD_DIM)
    v = v.reshape(nb, 1, N_HEADS, HEAD_DIM)
    return (y, jnp.stack(conv_states), jnp.stack(ffn_states), k, v), wb


def kernel(x_prompt, x_sample, state_conv_a, state_ffn_conv, cache_k, cache_v, page_table, meta_tokens, a_g_norm, a_w_pw1, a_w_dw, a_b_dw, a_g_cn, a_w_pw2, f_g_norm, f_w_gate, f_w_up, f_w_down, f_w_conv, f_b_conv, kv_g_norm, w_kv, g_k, b_g_norm, b_w_q, b_g_q, b_sb, b_w_o):
    wts = dict(a_g_norm=a_g_norm, a_w_pw1=a_w_pw1, a_w_dw=a_w_dw, a_b_dw=a_b_dw, a_g_cn=a_g_cn,
               a_w_pw2=a_w_pw2, f_g_norm=f_g_norm, f_w_gate=f_w_gate, f_w_up=f_w_up,
               f_w_down=f_w_down, f_w_conv=f_w_conv, f_b_conv=f_b_conv, kv_g_norm=kv_g_norm,
               w_kv=w_kv, g_k=g_k, b_g_norm=b_g_norm, b_w_q=b_w_q, b_g_q=b_g_q, b_sb=b_sb,
               b_w_o=b_w_o)
    (y_s, conv_s, ffn_s, k_s, v_s), wb = _sample_trunk(x_sample, state_conv_a, state_ffn_conv,
                                                       cache_k, cache_v, page_table, wts)
    bp = x_prompt.shape[0]
    meta = jnp.broadcast_to(meta_tokens[None], (bp, N_META, D_MODEL))
    xp = jnp.concatenate([meta, x_prompt], axis=1)
    y_p, conv_p, ffn_p, k_p, v_p = _prompt_trunk(xp, wts, wb)
    return (y_p, y_s, conv_p, ffn_p, k_p, v_p, conv_s, ffn_s, k_s, v_s)
```
